```python
import jax
import jax.numpy as jnp
from jax import lax
import numpy as np

D_MODEL = 1024
BATCH = 8
SEQ = 8192
DEPTH = 1

HEAD_DIM = 64
N_HEADS_MOBA = 8
N_HEADS_SB = 8
W_MOBA = N_HEADS_MOBA * HEAD_DIM
W_SB = N_HEADS_SB * HEAD_DIM
MOBA_BLOCK = 256
MOBA_TOPK = 3
Q_BLOCK = 128
D_FF = 2816
CONV_WIDTH = 3
RMS_EPS = 1e-6
IN_SIZES = (W_MOBA, W_MOBA, W_MOBA, W_SB, W_SB, W_SB, D_MODEL, D_MODEL)
D_IN = 3 * W_MOBA + 3 * W_SB + 2 * D_MODEL

kernel_name = 'hybrid_moba_stickbreak_convffn'


def _rmsnorm(x, g):
    xf = x.astype(jnp.float32)
    y = xf * lax.rsqrt(jnp.mean(xf * xf, axis=-1, keepdims=True) + RMS_EPS)
    return (y * g.astype(jnp.float32)).astype(x.dtype)


def _alibi_slopes(n):
    return jnp.exp2(-8.0 * jnp.arange(1, n + 1, dtype=jnp.float32) / n)


def _split_columns(proj):
    parts = []
    start = 0
    for size in IN_SIZES:
        parts.append(proj[..., start:start + size])
        start += size
    return parts


def _heads(t, n_heads):
    b, t_len, _ = t.shape
    return t.reshape(b, t_len, n_heads, HEAD_DIM).transpose(0, 2, 1, 3)


def _merge_heads(t):
    b, h, t_len, d = t.shape
    return t.transpose(0, 2, 1, 3).reshape(b, t_len, h * d)


def _moba_attention(q, k, v):
    n_b, n_h, t_len, dh = q.shape
    n_blk = -(-t_len // MOBA_BLOCK)
    pad = n_blk * MOBA_BLOCK - t_len
    n_qb = t_len // Q_BLOCK
    n_sel = min(MOBA_TOPK, n_blk)
    scale = dh ** -0.5
    slopes = _alibi_slopes(n_h)[:, None, None]
    pad_cfg = ((0, 0), (0, 0), (0, pad), (0, 0))
    kf = jnp.pad(k.astype(jnp.float32), pad_cfg).reshape(n_b, n_h, n_blk, MOBA_BLOCK, dh)
    vf = jnp.pad(v.astype(jnp.float32), pad_cfg).reshape(n_b, n_h, n_blk, MOBA_BLOCK, dh)
    k_mean = jnp.mean(kf, axis=3)
    blk_ids = jnp.arange(n_blk)
    in_blk = jnp.arange(MOBA_BLOCK)
    head_ids = jnp.arange(n_h)[:, None, None]

    def per_seq(args):
        q1, kb, vb, km = args
        qc_all = q1.astype(jnp.float32).reshape(n_h, n_qb, Q_BLOCK, dh).transpose(1, 0, 2, 3)

        def per_chunk(cargs):
            i, qc = cargs
            q_pos = i * Q_BLOCK + jnp.arange(Q_BLOCK)
            own = (i * Q_BLOCK) // MOBA_BLOCK
            gate = jnp.einsum('hqd,hnd->hqn', qc, km)
            gate = jnp.where(blk_ids < own, gate, -jnp.inf)
            _, sel = lax.top_k(gate, n_sel)
            sel_valid = (sel < own)[..., None]
            k_sel = kb[head_ids, sel]
            v_sel = vb[head_ids, sel]
            sel_pos = sel[..., None] * MOBA_BLOCK + in_blk
            s_sel = jnp.einsum('hqd,hqjkd->hqjk', qc, k_sel) * scale
            s_sel = s_sel - slopes[..., None] * (q_pos[None, :, None, None] - sel_pos)
            s_sel = jnp.where(sel_valid, s_sel, -jnp.inf).reshape(n_h, Q_BLOCK, n_sel * MOBA_BLOCK)
            k_own = lax.dynamic_index_in_dim(kb, own, axis=1, keepdims=False)
            v_own = lax.dynamic_index_in_dim(vb, own, axis=1, keepdims=False)
            own_pos = own * MOBA_BLOCK + in_blk
            s_own = jnp.einsum('hqd,hkd->hqk', qc, k_own) * scale
            s_own = s_own - slopes * (q_pos[:, None] - own_pos[None, :])
            s_own = jnp.where(own_pos[None, None, :] <= q_pos[None, :, None], s_own, -jnp.inf)
            p = jax.nn.softmax(jnp.concatenate([s_sel, s_own], axis=-1), axis=-1)
            p_sel = p[..., :n_sel * MOBA_BLOCK].reshape(n_h, Q_BLOCK, n_sel, MOBA_BLOCK)
            p_own = p[..., n_sel * MOBA_BLOCK:]
            return (jnp.einsum('hqjk,hqjkd->hqd', p_sel, v_sel)
                    + jnp.einsum('hqk,hkd->hqd', p_own, v_own))

        out = lax.map(per_chunk, (jnp.arange(n_qb), qc_all))
        return out.transpose(1, 0, 2, 3).reshape(n_h, t_len, dh)

    out = lax.map(per_seq, (q, kf, vf, k_mean))
    return out.astype(q.dtype)


def _stick_breaking_attention(q, k, v):
    n_b, n_h, t_len, dh = q.shape
    n_qb = t_len // Q_BLOCK
    scale = dh ** -0.5
    key_pos = jnp.arange(t_len)

    def per_seq(args):
        q1, k1, v1 = args
        kf = k1.astype(jnp.float32)
        vf = v1.astype(jnp.float32)
        qc_all = q1.astype(jnp.float32).reshape(n_h, n_qb, Q_BLOCK, dh).transpose(1, 0, 2, 3)

        def per_chunk(cargs):
            i, qc = cargs
            q_pos = i * Q_BLOCK + jnp.arange(Q_BLOCK)
            z = jnp.einsum('hqd,hkd->hqk', qc, kf) * scale
            strict = (key_pos[None, :] < q_pos[:, None])[None]
            log_beta = jax.nn.log_sigmoid(z)
            log_1m_beta = jnp.where(strict, jax.nn.log_sigmoid(-z), 0.0)
            tail = lax.cumsum(log_1m_beta, axis=2, reverse=True) - log_1m_beta
            w = jnp.where(strict, jnp.exp(log_beta + tail), 0.0)
            return jnp.einsum('hqk,hkd->hqd', w, vf)

        out = lax.map(per_chunk, (jnp.arange(n_qb), qc_all))
        return out.transpose(1, 0, 2, 3).reshape(n_h, t_len, dh)

    out = lax.map(per_seq, (q, k, v))
    return out.astype(q.dtype)


def setup_inputs(seed: int = 0) -> dict:
    key = jax.random.key(seed)
    ks = jax.random.split(key, 13)
    f32 = jnp.float32

    def nrm(k, shape, fan_in):
        return jax.random.normal(k, shape, f32) * (fan_in ** -0.5)

    return {
        'x': jax.random.normal(ks[0], (BATCH, SEQ, D_MODEL), f32),
        'g_mix': 1.0 + 0.02 * jax.random.normal(ks[1], (DEPTH, D_MODEL), f32),
        'w_in': nrm(ks[2], (DEPTH, D_MODEL, D_IN), D_MODEL),
        'w_proj_moba': nrm(ks[3], (DEPTH, W_MOBA, D_MODEL), W_MOBA),
        'w_proj_sb': nrm(ks[4], (DEPTH, W_SB, D_MODEL), W_SB),
        'w_out': nrm(ks[5], (DEPTH, D_MODEL, D_MODEL), D_MODEL),
        'g_ffn': 1.0 + 0.02 * jax.random.normal(ks[6], (DEPTH, D_MODEL), f32),
        'w_up': nrm(ks[7], (DEPTH, D_MODEL, 2 * D_FF), D_MODEL),
        'conv_w': nrm(ks[8], (DEPTH, CONV_WIDTH, 1, 2 * D_FF), CONV_WIDTH),
        'conv_b': 0.01 * jax.random.normal(ks[9], (DEPTH, 2 * D_FF), f32),
        'w_down': nrm(ks[10], (DEPTH, D_FF, D_MODEL), D_FF),
        'g_final': 1.0 + 0.02 * jax.random.normal(ks[11], (D_MODEL,), f32),
    }


def reference(x, g_mix, w_in, w_proj_moba, w_proj_sb, w_out, g_ffn, w_up, conv_w, conv_b, w_down, g_final):
    for layer in range(DEPTH):
        h = _rmsnorm(x, g_mix[layer])
        proj = h @ w_in[layer]
        q_a, k_a, v_a, q_b, k_b, v_b, gate_a, gate_b = _split_columns(proj)
        y_a = _merge_heads(_moba_attention(_heads(q_a, N_HEADS_MOBA), _heads(k_a, N_HEADS_MOBA),
                                           _heads(v_a, N_HEADS_MOBA)))
        y_b = _merge_heads(_stick_breaking_attention(_heads(q_b, N_HEADS_SB), _heads(k_b, N_HEADS_SB),
                                                     _heads(v_b, N_HEADS_SB)))
        mixed = (jax.nn.sigmoid(gate_a) * (y_a @ w_proj_moba[layer])
                 + jax.nn.sigmoid(gate_b) * (y_b @ w_proj_sb[layer]))
        x = x + mixed @ w_out[layer]
        h = _rmsnorm(x, g_ffn[layer])
        u = h @ w_up[layer]
        u = lax.conv_general_dilated(u, conv_w[layer], window_strides=(1,),
                                     padding=[(CONV_WIDTH - 1, 0)],
                                     dimension_numbers=('NWC', 'WIO', 'NWC'),
                                     feature_group_count=2 * D_FF) + conv_b[layer]
        a = u[..., :D_FF]
        b = u[..., D_FF:]
        x = x + (jax.nn.silu(a) * b) @ w_down[layer]
    return _rmsnorm(x, g_final)
```

```python
import functools

import jax
import jax.numpy as jnp
from jax import lax
from jax.experimental import pallas as pl
from jax.experimental.pallas import tpu as pltpu

F32 = jnp.float32
BF16 = jnp.bfloat16

HEAD_DIM = 64
N_HEADS = 8
W_ATT = N_HEADS * HEAD_DIM
HEADS_PER_STEP = 2
LANES = 128
KEY_BLOCK = 256
Q_TILE = 256
MOBA_TOPK = 3
RMS_EPS = 1e-6
CONV_WIDTH = 3
FF_CHUNK = 256
ROW_TILE = 512
TAIL_ROWS = 8
SB_LOG_WEIGHT_FLOOR = -110.0
VMEM_LIMIT = 56 * 1024 * 1024


def _dot(a, b):
    return jnp.dot(a, b, preferred_element_type=F32)


def _rms_normalize(x, g):
    return x * lax.rsqrt(jnp.mean(x * x, axis=-1, keepdims=True) + RMS_EPS) * g


def _inproj_kernel(x_ref, g_ref, w_ref, qa_ref, ka_ref, va_ref, qb_ref, kb_ref, vb_ref, gate_ref):
    h = _rms_normalize(x_ref[0], g_ref[...]).astype(BF16)
    scale = HEAD_DIM ** -0.5

    def proj(lo, hi):
        return _dot(h, w_ref[:, lo:hi])

    def put_blocks(ref, t):
        for j in range(t.shape[1] // KEY_BLOCK):
            ref[0, j] = t[:, j * KEY_BLOCK:(j + 1) * KEY_BLOCK].astype(BF16)

    w = W_ATT
    qa_ref[0] = (proj(0, w) * scale).T.astype(BF16)
    ka_ref[0] = proj(w, 2 * w).astype(BF16)
    put_blocks(va_ref, proj(2 * w, 3 * w).T)
    qb_ref[0] = (proj(3 * w, 4 * w) * scale).T.astype(BF16)
    kb_ref[0] = proj(4 * w, 5 * w).astype(BF16)
    put_blocks(vb_ref, proj(5 * w, 6 * w).T)
    gate_ref[0] = jax.nn.sigmoid(proj(6 * w, w_ref.shape[1])).astype(BF16)


def _inproj(x, g, w_in):
    b, t, d = x.shape
    d_in = w_in.shape[1]
    tm = ROW_TILE
    nblk = t // KEY_BLOCK
    qt_shape = jax.ShapeDtypeStruct((b, W_ATT, t), BF16)
    k_shape = jax.ShapeDtypeStruct((b, t, W_ATT), BF16)
    vt_shape = jax.ShapeDtypeStruct((b, nblk, W_ATT, KEY_BLOCK), BF16)
    gate_shape = jax.ShapeDtypeStruct((b, t, d_in - 6 * W_ATT), BF16)
    qt_spec = pl.BlockSpec((1, W_ATT, tm), lambda bi, i: (bi, 0, i))
    k_spec = pl.BlockSpec((1, tm, W_ATT), lambda bi, i: (bi, i, 0))
    vt_spec = pl.BlockSpec((1, tm // KEY_BLOCK, W_ATT, KEY_BLOCK), lambda bi, i: (bi, i, 0, 0))
    return pl.pallas_call(
        _inproj_kernel,
        grid=(b, t // tm),
        in_specs=[
            pl.BlockSpec((1, tm, d), lambda bi, i: (bi, i, 0)),
            pl.BlockSpec((1, d), lambda bi, i: (0, 0)),
            pl.BlockSpec((d, d_in), lambda bi, i: (0, 0)),
        ],
        out_specs=[qt_spec, k_spec, vt_spec, qt_spec, k_spec, vt_spec,
                   pl.BlockSpec((1, tm, d_in - 6 * W_ATT), lambda bi, i: (bi, i, 0))],
        out_shape=[qt_shape, k_shape, vt_shape, qt_shape, k_shape, vt_shape, gate_shape],
        compiler_params=pltpu.CompilerParams(
            dimension_semantics=("parallel", "parallel"), vmem_limit_bytes=VMEM_LIMIT),
        name="inproj",
    )(x, g.reshape(1, d), w_in)


def _head_queries(qt, h):
    row = lax.broadcasted_iota(jnp.int32, qt.shape, 0)
    keep = (row >= h * HEAD_DIM) & (row < (h + 1) * HEAD_DIM)
    return jnp.where(keep, qt.astype(F32), 0.0).astype(BF16)


def _key_block(k_ref, j):
    return k_ref[0, pl.ds(pl.multiple_of(j * KEY_BLOCK, KEY_BLOCK), KEY_BLOCK), :]


def _value_block(vt_ref, j, h):
    return vt_ref[0, j, h * HEAD_DIM:(h + 1) * HEAD_DIM, :]


def _moba_kernel(slopes_ref, qt_ref, k_ref, vt_ref, y_ref, kmean_ref, selbias_ref, *, nblk):
    hp = pl.program_id(1)
    qi = pl.program_id(2)
    tq = Q_TILE

    @pl.when(qi == 0)
    def _():
        def body(j, c):
            blk = _key_block(k_ref, j).astype(F32)
            kmean_ref[pl.ds(j, 1), :] = jnp.sum(blk, axis=0, keepdims=True) * (1.0 / KEY_BLOCK)
            return c
        lax.fori_loop(0, nblk, body, 0)

    km = kmean_ref[...]
    km_hi = km.astype(BF16)
    km_r = km - km_hi.astype(F32)
    km_mid = km_r.astype(BF16)
    km_lo = (km_r - km_mid.astype(F32)).astype(BF16)

    qt = qt_ref[0]
    blk_id = lax.broadcasted_iota(jnp.int32, (nblk, tq), 0).astype(F32)
    key_in_blk = lax.broadcasted_iota(jnp.int32, (KEY_BLOCK, tq), 0)
    qry_in_blk = lax.broadcasted_iota(jnp.int32, (KEY_BLOCK, tq), 1)
    rel = (key_in_blk - qry_in_blk).astype(F32)
    own = qi.astype(F32)
    neg_inf = -jnp.inf

    outs = []
    for h in range(HEADS_PER_STEP):
        slope = slopes_ref[hp * HEADS_PER_STEP + h]
        qh = _head_queries(qt, h)
        gate = _dot(km_hi, qh) + _dot(km_mid, qh) + _dot(km_lo, qh)
        gate = jnp.where(blk_id < own, gate, neg_inf)
        chosen = jnp.zeros((nblk, tq), F32)
        for _ in range(min(MOBA_TOPK, nblk)):
            best = jnp.max(gate, axis=0, keepdims=True)
            first = jnp.min(jnp.where(gate == best, blk_id, float(nblk)), axis=0, keepdims=True)
            hit = blk_id == first
            chosen = jnp.where(hit & (best > neg_inf), 1.0, chosen)
            gate = jnp.where(hit, neg_inf, gate)
        selbias_ref[h] = jnp.where(chosen > 0.0, slope * KEY_BLOCK * (blk_id - own), neg_inf)
        alibi = slope * rel

        s = _dot(_key_block(k_ref, qi), qh) + alibi
        s = jnp.where(key_in_blk <= qry_in_blk, s, neg_inf)
        m = jnp.max(s, axis=0, keepdims=True)
        p = jnp.exp(s - m)
        l = jnp.sum(p, axis=0, keepdims=True)
        acc = _dot(_value_block(vt_ref, qi, h), p.astype(BF16))

        def body(j, carry, h=h, qh=qh, alibi=alibi):
            m, l, acc = carry
            s = _dot(_key_block(k_ref, j), qh) + alibi + selbias_ref[h, pl.ds(j, 1), :]
            m_new = jnp.maximum(m, jnp.max(s, axis=0, keepdims=True))
            alpha = jnp.exp(m - m_new)
            p = jnp.exp(s - m_new)
            l = alpha * l + jnp.sum(p, axis=0, keepdims=True)
            acc = alpha * acc + _dot(_value_block(vt_ref, j, h), p.astype(BF16))
            return m_new, l, acc

        m, l, acc = lax.fori_loop(0, qi, body, (m, l, acc))
        outs.append(acc / l)

    y_ref[0] = jnp.concatenate(outs, axis=0).T.astype(BF16)


def _attention_specs(b, t):
    nblk = t // KEY_BLOCK
    width = HEADS_PER_STEP * HEAD_DIM
    in_specs = [
        pl.BlockSpec((1, width, Q_TILE), lambda bi, hp, qi: (bi, hp, qi)),
        pl.BlockSpec((1, t, width), lambda bi, hp, qi: (bi, 0, hp)),
        pl.BlockSpec((1, nblk, width, KEY_BLOCK), lambda bi, hp, qi: (bi, 0, hp, 0)),
    ]
    out_spec = pl.BlockSpec((1, Q_TILE, width), lambda bi, hp, qi: (bi, qi, hp))
    grid = (b, N_HEADS // HEADS_PER_STEP, t // Q_TILE)
    return grid, in_specs, out_spec


def _moba_attention(slopes, qt, k, vt):
    b, t, _ = k.shape
    nblk = t // KEY_BLOCK
    grid, in_specs, out_spec = _attention_specs(b, t)
    return pl.pallas_call(
        functools.partial(_moba_kernel, nblk=nblk),
        grid=grid,
        in_specs=[pl.BlockSpec(memory_space=pltpu.SMEM)] + in_specs,
        out_specs=out_spec,
        out_shape=jax.ShapeDtypeStruct((b, t, W_ATT), BF16),
        scratch_shapes=[pltpu.VMEM((nblk, HEADS_PER_STEP * HEAD_DIM), F32),
                        pltpu.VMEM((HEADS_PER_STEP, nblk, Q_TILE), F32)],
        compiler_params=pltpu.CompilerParams(
            dimension_semantics=("parallel", "parallel", "arbitrary"), vmem_limit_bytes=VMEM_LIMIT),
        name="moba_attention",
    )(slopes, qt, k, vt)


def _sb_block(z, later_keys, vblk, carry_in, strict):
    softplus = jnp.maximum(z, 0.0) + jnp.log1p(jnp.exp(-jnp.abs(z)))
    log_1m_beta = -softplus
    if strict is not None:
        log_1m_beta = jnp.where(strict, log_1m_beta, 0.0)
    log_beta = z - softplus
    hi = log_1m_beta.astype(BF16)
    lo = (log_1m_beta - hi.astype(F32)).astype(BF16)
    tail = _dot(later_keys, hi) + _dot(later_keys, lo) + carry_in
    w = jnp.exp(log_beta + tail)
    if strict is not None:
        w = jnp.where(strict, w, 0.0)
    carry_out = tail[0:1, :] + log_1m_beta[0:1, :]
    return _dot(vblk, w.astype(BF16)), carry_out


def _sb_kernel(qt_ref, k_ref, vt_ref, y_ref):
    qi = pl.program_id(2)
    tq = Q_TILE
    qt = qt_ref[0]
    key_in_blk = lax.broadcasted_iota(jnp.int32, (KEY_BLOCK, tq), 0)
    qry_in_blk = lax.broadcasted_iota(jnp.int32, (KEY_BLOCK, tq), 1)
    strict = key_in_blk < qry_in_blk
    r = lax.broadcasted_iota(jnp.int32, (KEY_BLOCK, KEY_BLOCK), 0)
    c = lax.broadcasted_iota(jnp.int32, (KEY_BLOCK, KEY_BLOCK), 1)
    later_keys = jnp.where(c > r, 1.0, 0.0).astype(BF16)

    outs = []
    for h in range(HEADS_PER_STEP):
        qh = _head_queries(qt, h)
        z = _dot(_key_block(k_ref, qi), qh)
        acc, carry = _sb_block(z, later_keys, _value_block(vt_ref, qi, h), jnp.zeros((1, tq), F32), strict)

        def cond(state):
            j, carry_max, _, _ = state
            return (j >= 0) & (carry_max > SB_LOG_WEIGHT_FLOOR)

        def body(state, h=h, qh=qh):
            j, _, carry, acc = state
            z = _dot(_key_block(k_ref, j), qh)
            contrib, carry = _sb_block(z, later_keys, _value_block(vt_ref, j, h), carry, None)
            return j - 1, jnp.max(carry), carry, acc + contrib

        _, _, _, acc = lax.while_loop(cond, body, (qi - 1, jnp.max(carry), carry, acc))
        outs.append(acc)

    y_ref[0] = jnp.concatenate(outs, axis=0).T.astype(BF16)


def _sb_attention(qt, k, vt):
    b, t, _ = k.shape
    grid, in_specs, out_spec = _attention_specs(b, t)
    return pl.pallas_call(
        _sb_kernel,
        grid=grid,
        in_specs=in_specs,
        out_specs=out_spec,
        out_shape=jax.ShapeDtypeStruct((b, t, W_ATT), BF16),
        compiler_params=pltpu.CompilerParams(
            dimension_semantics=("parallel", "parallel", "arbitrary"), vmem_limit_bytes=VMEM_LIMIT),
        name="stickbreak_attention",
    )(qt, k, vt)


def _mix_kernel(x_ref, ya_ref, yb_ref, gate_ref, wa_ref, wb_ref, wo_ref, o_ref):
    d = x_ref.shape[1]
    ga = gate_ref[:, :d].astype(F32)
    gb = gate_ref[:, d:].astype(F32)
    mixed = ga * _dot(ya_ref[...], wa_ref[...]) + gb * _dot(yb_ref[...], wb_ref[...])
    o_ref[...] = x_ref[...] + _dot(mixed.astype(BF16), wo_ref[...])


def _mix(x2, ya, yb, gate, wa, wb, wo):
    n, d = x2.shape
    tm = ROW_TILE
    row = lambda width: pl.BlockSpec((tm, width), lambda i: (i, 0))
    full = lambda a: pl.BlockSpec(a.shape, lambda i: (0, 0))
    return pl.pallas_call(
        _mix_kernel,
        grid=(n // tm,),
        in_specs=[row(d), row(W_ATT), row(W_ATT), row(2 * d), full(wa), full(wb), full(wo)],
        out_specs=row(d),
        out_shape=jax.ShapeDtypeStruct((n, d), F32),
        compiler_params=pltpu.CompilerParams(
            dimension_semantics=("parallel",), vmem_limit_bytes=VMEM_LIMIT),
        name="mix_outproj",
    )(x2, ya, yb, gate, wa, wb, wo)


def _ffn_kernel(x_ref, g_ref, wup_ref, cw_ref, cb_ref, wdown_ref, gfin_ref, o_ref, tail_ref, act_ref,
                *, tiles_per_seq, final_norm):
    tm = x_ref.shape[0]
    d_ff = wdown_ref.shape[0]

    @pl.when(pl.program_id(0) % tiles_per_seq == 0)
    def _():
        tail_ref[...] = jnp.zeros_like(tail_ref)

    x = x_ref[...]
    h = _rms_normalize(x, g_ref[...]).astype(BF16)
    row = lax.broadcasted_iota(jnp.int32, (tm, FF_CHUNK), 0)

    def conv_chunk(lo):
        u = _dot(h, wup_ref[:, lo:lo + FF_CHUNK])
        prev = tail_ref[:, lo:lo + FF_CHUNK]
        tail_ref[:, lo:lo + FF_CHUNK] = u[tm - TAIL_ROWS:, :]
        p1 = prev[TAIL_ROWS - 1:TAIL_ROWS, :]
        p2 = prev[TAIL_ROWS - 2:TAIL_ROWS - 1, :]
        u1 = jnp.where(row == 0, p1, pltpu.roll(u, 1, 0))
        u2 = jnp.where(row == 0, p2, jnp.where(row == 1, p1, pltpu.roll(u, 2, 0)))
        cw = cw_ref[:, lo:lo + FF_CHUNK]
        return cw[0:1] * u2 + cw[1:2] * u1 + cw[2:3] * u + cb_ref[:, lo:lo + FF_CHUNK]

    for c in range(d_ff // FF_CHUNK):
        a = conv_chunk(c * FF_CHUNK)
        bgate = conv_chunk(d_ff + c * FF_CHUNK)
        act_ref[:, c * FF_CHUNK:(c + 1) * FF_CHUNK] = (a * jax.nn.sigmoid(a) * bgate).astype(BF16)

    y = x + _dot(act_ref[...], wdown_ref[...])
    if final_norm:
        y = _rms_normalize(y, gfin_ref[...])
    o_ref[...] = y


def _ffn(x2, g, wup, cw, cb, wdown, gfin, *, seq_len, final_norm):
    n, d = x2.shape
    d_ff = wdown.shape[0]
    tm = ROW_TILE
    row = pl.BlockSpec((tm, d), lambda i: (i, 0))
    full = lambda a: pl.BlockSpec(a.shape, lambda i: (0, 0))
    return pl.pallas_call(
        functools.partial(_ffn_kernel, tiles_per_seq=seq_len // tm, final_norm=final_norm),
        grid=(n // tm,),
        in_specs=[row, full(g), full(wup), full(cw), full(cb), full(wdown), full(gfin)],
        out_specs=row,
        out_shape=jax.ShapeDtypeStruct((n, d), F32),
        scratch_shapes=[pltpu.VMEM((TAIL_ROWS, 2 * d_ff), F32), pltpu.VMEM((tm, d_ff), BF16)],
        compiler_params=pltpu.CompilerParams(
            dimension_semantics=("arbitrary",), vmem_limit_bytes=VMEM_LIMIT),
        name="conv_ffn",
    )(x2, g, wup, cw, cb, wdown, gfin)


def kernel(x, g_mix, w_in, w_proj_moba, w_proj_sb, w_out, g_ffn, w_up, conv_w, conv_b, w_down, g_final):
    b, t, d = x.shape
    depth = w_in.shape[0]
    assert t % ROW_TILE == 0 and ROW_TILE % KEY_BLOCK == 0 and Q_TILE == KEY_BLOCK
    assert w_up.shape[2] % (2 * FF_CHUNK) == 0 and conv_w.shape[1] == CONV_WIDTH
    slopes = jnp.exp2(-8.0 * jnp.arange(1, N_HEADS + 1, dtype=F32) / N_HEADS)
    gfin = g_final.reshape(1, d)
    for layer in range(depth):
        qta, ka, vta, qtb, kb, vtb, gate = _inproj(x, g_mix[layer], w_in[layer].astype(BF16))
        ya = _moba_attention(slopes, qta, ka, vta)
        yb = _sb_attention(qtb, kb, vtb)
        x1 = _mix(x.reshape(b * t, d), ya.reshape(b * t, W_ATT), yb.reshape(b * t, W_ATT),
                  gate.reshape(b * t, 2 * d), w_proj_moba[layer].astype(BF16),
                  w_proj_sb[layer].astype(BF16), w_out[layer].astype(BF16))
        x = _ffn(x1, g_ffn[layer].reshape(1, d), w_up[layer].astype(BF16),
                 conv_w[layer].reshape(CONV_WIDTH, -1), conv_b[layer].reshape(1, -1),
                 w_down[layer].astype(BF16), gfin, seq_len=t,
                 final_norm=(layer == depth - 1)).reshape(b, t, d)
    return x
```

```python
import functools

import jax
import jax.numpy as jnp
from jax import lax
from jax.experimental import pallas as pl
from jax.experimental.pallas import tpu as pltpu

F32 = jnp.float32
BF16 = jnp.bfloat16

HEAD_DIM = 64
N_HEADS = 8
W_ATT = N_HEADS * HEAD_DIM
HEADS_PER_STEP = 2
LANES = 128
KEY_BLOCK = 256
Q_TILE = 256
MOBA_TOPK = 3
MOBA_UNROLL = 1
LOG2E = 1.4426950408889634
RMS_EPS = 1e-6
CONV_WIDTH = 3
FF_CHUNK = 256
ROW_TILE = 512
TAIL_ROWS = 8
SB_LOG_WEIGHT_FLOOR = -110.0
VMEM_LIMIT = 56 * 1024 * 1024


def _dot(a, b):
    return jnp.dot(a, b, preferred_element_type=F32)


def _rms_normalize(x, g):
    return x * lax.rsqrt(jnp.mean(x * x, axis=-1, keepdims=True) + RMS_EPS) * g


def _inproj_kernel(x_ref, g_ref, w_ref, qa_ref, ka_ref, va_ref, qb_ref, kb_ref, vb_ref, gate_ref):
    h = _rms_normalize(x_ref[0], g_ref[...]).astype(BF16)
    scale = HEAD_DIM ** -0.5

    def proj(lo, hi):
        return _dot(h, w_ref[:, lo:hi])

    def put_blocks(ref, t):
        for j in range(t.shape[1] // KEY_BLOCK):
            ref[0, j] = t[:, j * KEY_BLOCK:(j + 1) * KEY_BLOCK].astype(BF16)

    w = W_ATT
    qa_ref[0] = (proj(0, w) * (scale * LOG2E)).T.astype(BF16)
    ka_ref[0] = proj(w, 2 * w).astype(BF16)
    put_blocks(va_ref, proj(2 * w, 3 * w).T)
    qb_ref[0] = (proj(3 * w, 4 * w) * scale).T.astype(BF16)
    kb_ref[0] = proj(4 * w, 5 * w).astype(BF16)
    put_blocks(vb_ref, proj(5 * w, 6 * w).T)
    gate_ref[0] = jax.nn.sigmoid(proj(6 * w, w_ref.shape[1])).astype(BF16)


def _inproj(x, g, w_in):
    b, t, d = x.shape
    d_in = w_in.shape[1]
    tm = ROW_TILE
    nblk = t // KEY_BLOCK
    qt_shape = jax.ShapeDtypeStruct((b, W_ATT, t), BF16)
    k_shape = jax.ShapeDtypeStruct((b, t, W_ATT), BF16)
    vt_shape = jax.ShapeDtypeStruct((b, nblk, W_ATT, KEY_BLOCK), BF16)
    gate_shape = jax.ShapeDtypeStruct((b, t, d_in - 6 * W_ATT), BF16)
    qt_spec = pl.BlockSpec((1, W_ATT, tm), lambda bi, i: (bi, 0, i))
    k_spec = pl.BlockSpec((1, tm, W_ATT), lambda bi, i: (bi, i, 0))
    vt_spec = pl.BlockSpec((1, tm // KEY_BLOCK, W_ATT, KEY_BLOCK), lambda bi, i: (bi, i, 0, 0))
    return pl.pallas_call(
        _inproj_kernel,
        grid=(b, t // tm),
        in_specs=[
            pl.BlockSpec((1, tm, d), lambda bi, i: (bi, i, 0)),
            pl.BlockSpec((1, d), lambda bi, i: (0, 0)),
            pl.BlockSpec((d, d_in), lambda bi, i: (0, 0)),
        ],
        out_specs=[qt_spec, k_spec, vt_spec, qt_spec, k_spec, vt_spec,
                   pl.BlockSpec((1, tm, d_in - 6 * W_ATT), lambda bi, i: (bi, i, 0))],
        out_shape=[qt_shape, k_shape, vt_shape, qt_shape, k_shape, vt_shape, gate_shape],
        compiler_params=pltpu.CompilerParams(
            dimension_semantics=("parallel", "parallel"), vmem_limit_bytes=VMEM_LIMIT),
        name="inproj",
    )(x, g.reshape(1, d), w_in)


def _head_queries(qt, h):
    row = lax.broadcasted_iota(jnp.int32, qt.shape, 0)
    keep = (row >= h * HEAD_DIM) & (row < (h + 1) * HEAD_DIM)
    return jnp.where(keep, qt.astype(F32), 0.0).astype(BF16)


def _key_block(k_ref, j):
    return k_ref[0, pl.ds(pl.multiple_of(j * KEY_BLOCK, KEY_BLOCK), KEY_BLOCK), :]


def _value_block(vt_ref, j, h):
    return vt_ref[0, j, h * HEAD_DIM:(h + 1) * HEAD_DIM, :]


def _moba_kernel(slopes_ref, qt_ref, k_ref, vt_ref, y_ref, kmean_ref, rowbias_ref, tile_ref, s_ref, *, nblk):
    hp = pl.program_id(1)
    qi = pl.program_id(2)
    tq = Q_TILE

    @pl.when(qi == 0)
    def _():
        def body(j, c):
            blk = _key_block(k_ref, j).astype(F32)
            kmean_ref[pl.ds(j, 1), :] = jnp.sum(blk, axis=0, keepdims=True) * (1.0 / KEY_BLOCK)
            return c
        lax.fori_loop(0, nblk, body, 0)

    km = kmean_ref[...]
    km_hi = km.astype(BF16)
    km_r = km - km_hi.astype(F32)
    km_mid = km_r.astype(BF16)
    km_lo = (km_r - km_mid.astype(F32)).astype(BF16)

    qt = qt_ref[0]
    blk_id = lax.broadcasted_iota(jnp.int32, (nblk, tq), 0).astype(F32)
    key_in_blk = lax.broadcasted_iota(jnp.int32, (KEY_BLOCK, tq), 0)
    qry_in_blk = lax.broadcasted_iota(jnp.int32, (KEY_BLOCK, tq), 1)
    rel = (key_in_blk - qry_in_blk).astype(F32)
    own = qi.astype(F32)
    neg_inf = -jnp.inf

    heads = range(HEADS_PER_STEP)
    qhs = []
    for h in heads:
        slope2 = slopes_ref[hp * HEADS_PER_STEP + h] * LOG2E
        qh = _head_queries(qt, h)
        gate = _dot(km_hi, qh) + _dot(km_mid, qh) + _dot(km_lo, qh)
        gate = jnp.where(blk_id < own, gate, neg_inf)
        chosen = jnp.where(blk_id == own, 1.0, 0.0)
        for _ in range(min(MOBA_TOPK, nblk)):
            best = jnp.max(gate, axis=0, keepdims=True)
            first = jnp.min(jnp.where(gate == best, blk_id, float(nblk)), axis=0, keepdims=True)
            hit = blk_id == first
            chosen = jnp.where(hit & (best > neg_inf), 1.0, chosen)
            gate = jnp.where(hit, neg_inf, gate)
        rowbias_ref[h, :nblk] = jnp.where(chosen > 0.0, slope2 * KEY_BLOCK * (blk_id - own), neg_inf)
        rowbias_ref[h, nblk:] = jnp.full((TAIL_ROWS, tq), neg_inf, F32)
        alibi = slope2 * rel
        tile_ref[h, 0] = alibi
        tile_ref[h, 1] = jnp.where(key_in_blk <= qry_in_blk, alibi, neg_inf)
        qhs.append(qh)

    def position(t):
        t = jnp.asarray(t, jnp.int32)
        is_own = t == 0
        kidx = jnp.where(is_own, qi, jnp.minimum(t - 1, jnp.maximum(qi - 1, 0)))
        ridx = jnp.where(is_own, qi, jnp.where(t <= qi, t - 1, nblk))
        return is_own.astype(jnp.int32), kidx, ridx

    def score_stage(i, slot):
        maxes = []
        for g in range(MOBA_UNROLL):
            kind, kidx, ridx = position(i * MOBA_UNROLL + g)
            kblk = _key_block(k_ref, kidx)
            for h in heads:
                s = _dot(kblk, qhs[h]) + tile_ref[h, kind]
                s_ref[slot, g * HEADS_PER_STEP + h] = s
                maxes.append(jnp.max(s, axis=0, keepdims=True) + rowbias_ref[h, pl.ds(ridx, 1), :])
        return tuple(maxes)

    def value_stage(i, slot, maxes, states):
        new_states = []
        for h in heads:
            m, l, acc = states[h]
            m_new = m
            for g in range(MOBA_UNROLL):
                m_new = jnp.maximum(m_new, maxes[g * HEADS_PER_STEP + h])
            alpha = jnp.exp2(m - m_new)
            l = alpha * l
            acc = alpha * acc
            for g in range(MOBA_UNROLL):
                _, kidx, ridx = position(i * MOBA_UNROLL + g)
                shift = m_new - rowbias_ref[h, pl.ds(ridx, 1), :]
                p = jnp.exp2(s_ref[slot, g * HEADS_PER_STEP + h] - shift)
                l = l + jnp.sum(p, axis=0, keepdims=True)
                acc = acc + _dot(_value_block(vt_ref, kidx, h), p.astype(BF16))
            new_states.append((m_new, l, acc))
        return tuple(new_states)

    def body(ii, carry):
        maxes0, states = carry
        i = 2 * ii
        maxes1 = score_stage(i + 1, 1)
        states = value_stage(i, 0, maxes0, states)
        maxes2 = score_stage(i + 2, 0)
        states = value_stage(i + 1, 1, maxes1, states)
        return maxes2, states

    n_steps = (qi + MOBA_UNROLL) // MOBA_UNROLL
    n_pairs = (n_steps - 1) // 2
    init = tuple((jnp.full((1, tq), neg_inf, F32), jnp.zeros((1, tq), F32), jnp.zeros((HEAD_DIM, tq), F32))
                 for _ in heads)
    maxes0, states = lax.fori_loop(0, n_pairs, body, (score_stage(0, 0), init))
    last = 2 * n_pairs
    maxes1 = score_stage(last + 1, 1)
    states = value_stage(last, 0, maxes0, states)
    states = value_stage(last + 1, 1, maxes1, states)
    outs = [acc / l for _, l, acc in states]

    y_ref[0] = jnp.concatenate(outs, axis=0).T.astype(BF16)


def _attention_specs(b, t):
    nblk = t // KEY_BLOCK
    width = HEADS_PER_STEP * HEAD_DIM
    in_specs = [
        pl.BlockSpec((1, width, Q_TILE), lambda bi, hp, qi: (bi, hp, qi)),
        pl.BlockSpec((1, t, width), lambda bi, hp, qi: (bi, 0, hp)),
        pl.BlockSpec((1, nblk, width, KEY_BLOCK), lambda bi, hp, qi: (bi, 0, hp, 0)),
    ]
    out_spec = pl.BlockSpec((1, Q_TILE, width), lambda bi, hp, qi: (bi, qi, hp))
    grid = (b, N_HEADS // HEADS_PER_STEP, t // Q_TILE)
    return grid, in_specs, out_spec


def _moba_attention(slopes, qt, k, vt):
    b, t, _ = k.shape
    nblk = t // KEY_BLOCK
    grid, in_specs, out_spec = _attention_specs(b, t)
    return pl.pallas_call(
        functools.partial(_moba_kernel, nblk=nblk),
        grid=grid,
        in_specs=[pl.BlockSpec(memory_space=pltpu.SMEM)] + in_specs,
        out_specs=out_spec,
        out_shape=jax.ShapeDtypeStruct((b, t, W_ATT), BF16),
        scratch_shapes=[pltpu.VMEM((nblk, HEADS_PER_STEP * HEAD_DIM), F32),
                        pltpu.VMEM((HEADS_PER_STEP, nblk + TAIL_ROWS, Q_TILE), F32),
                        pltpu.VMEM((HEADS_PER_STEP, 2, KEY_BLOCK, Q_TILE), F32),
                        pltpu.VMEM((2, MOBA_UNROLL * HEADS_PER_STEP, KEY_BLOCK, Q_TILE), F32)],
        compiler_params=pltpu.CompilerParams(
            dimension_semantics=("parallel", "parallel", "arbitrary"), vmem_limit_bytes=VMEM_LIMIT),
        name="moba_attention",
    )(slopes, qt, k, vt)


def _sb_block(z, later_keys, vblk, carry_in, strict):
    softplus = jnp.maximum(z, 0.0) + jnp.log1p(jnp.exp(-jnp.abs(z)))
    log_1m_beta = -softplus
    if strict is not None:
        log_1m_beta = jnp.where(strict, log_1m_beta, 0.0)
    log_beta = z - softplus
    hi = log_1m_beta.astype(BF16)
    lo = (log_1m_beta - hi.astype(F32)).astype(BF16)
    tail = _dot(later_keys, hi) + _dot(later_keys, lo) + carry_in
    w = jnp.exp(log_beta + tail)
    if strict is not None:
        w = jnp.where(strict, w, 0.0)
    carry_out = tail[0:1, :] + log_1m_beta[0:1, :]
    return _dot(vblk, w.astype(BF16)), carry_out


def _sb_kernel(qt_ref, k_ref, vt_ref, y_ref):
    qi = pl.program_id(2)
    tq = Q_TILE
    qt = qt_ref[0]
    key_in_blk = lax.broadcasted_iota(jnp.int32, (KEY_BLOCK, tq), 0)
    qry_in_blk = lax.broadcasted_iota(jnp.int32, (KEY_BLOCK, tq), 1)
    strict = key_in_blk < qry_in_blk
    r = lax.broadcasted_iota(jnp.int32, (KEY_BLOCK, KEY_BLOCK), 0)
    c = lax.broadcasted_iota(jnp.int32, (KEY_BLOCK, KEY_BLOCK), 1)
    later_keys = jnp.where(c > r, 1.0, 0.0).astype(BF16)

    heads = range(HEADS_PER_STEP)
    qhs = [_head_queries(qt, h) for h in heads]

    def block(h, j, carry, mask):
        z = _dot(_key_block(k_ref, j), qhs[h])
        return _sb_block(z, later_keys, _value_block(vt_ref, j, h), carry, mask)

    def carry_max(carries):
        return jnp.max(jnp.maximum(carries[0], carries[1]))

    def cond(state):
        j, cmax, _, _ = state
        return (j >= 0) & (cmax > SB_LOG_WEIGHT_FLOOR)

    def body(state):
        j, _, carries, accs = state
        res = [block(h, j, carries[h], None) for h in heads]
        carries = tuple(c for _, c in res)
        accs = tuple(accs[h] + res[h][0] for h in heads)
        return j - 1, carry_max(carries), carries, accs

    res = [block(h, qi, jnp.zeros((1, tq), F32), strict) for h in heads]
    accs = tuple(a for a, _ in res)
    carries = tuple(c for _, c in res)
    _, _, _, accs = lax.while_loop(cond, body, (qi - 1, carry_max(carries), carries, accs))

    y_ref[0] = jnp.concatenate(accs, axis=0).T.astype(BF16)


def _sb_attention(qt, k, vt):
    b, t, _ = k.shape
    grid, in_specs, out_spec = _attention_specs(b, t)
    return pl.pallas_call(
        _sb_kernel,
        grid=grid,
        in_specs=in_specs,
        out_specs=out_spec,
        out_shape=jax.ShapeDtypeStruct((b, t, W_ATT), BF16),
        compiler_params=pltpu.CompilerParams(
            dimension_semantics=("parallel", "parallel", "arbitrary"), vmem_limit_bytes=VMEM_LIMIT),
        name="stickbreak_attention",
    )(qt, k, vt)


def _mix_kernel(x_ref, ya_ref, yb_ref, gate_ref, wa_ref, wb_ref, wo_ref, o_ref):
    d = x_ref.shape[1]
    ga = gate_ref[:, :d].astype(F32)
    gb = gate_ref[:, d:].astype(F32)
    mixed = ga * _dot(ya_ref[...], wa_ref[...]) + gb * _dot(yb_ref[...], wb_ref[...])
    o_ref[...] = x_ref[...] + _dot(mixed.astype(BF16), wo_ref[...])


def _mix(x2, ya, yb, gate, wa, wb, wo):
    n, d = x2.shape
    tm = ROW_TILE
    row = lambda width: pl.BlockSpec((tm, width), lambda i: (i, 0))
    full = lambda a: pl.BlockSpec(a.shape, lambda i: (0, 0))
    return pl.pallas_call(
        _mix_kernel,
        grid=(n // tm,),
        in_specs=[row(d), row(W_ATT), row(W_ATT), row(2 * d), full(wa), full(wb), full(wo)],
        out_specs=row(d),
        out_shape=jax.ShapeDtypeStruct((n, d), F32),
        compiler_params=pltpu.CompilerParams(
            dimension_semantics=("parallel",), vmem_limit_bytes=VMEM_LIMIT),
        name="mix_outproj",
    )(x2, ya, yb, gate, wa, wb, wo)


def _ffn_kernel(x_ref, g_ref, wup_ref, cw_ref, cb_ref, wdown_ref, gfin_ref, o_ref, tail_ref, act_ref,
                *, tiles_per_seq, final_norm):
    tm = x_ref.shape[0]
    d_ff = wdown_ref.shape[0]

    @pl.when(pl.program_id(0) % tiles_per_seq == 0)
    def _():
        tail_ref[...] = jnp.zeros_like(tail_ref)

    x = x_ref[...]
    h = _rms_normalize(x, g_ref[...]).astype(BF16)
    row = lax.broadcasted_iota(jnp.int32, (tm, FF_CHUNK), 0)

    def conv_chunk(lo):
        u = _dot(h, wup_ref[:, lo:lo + FF_CHUNK])
        prev = tail_ref[:, lo:lo + FF_CHUNK]
        tail_ref[:, lo:lo + FF_CHUNK] = u[tm - TAIL_ROWS:, :]
        p1 = prev[TAIL_ROWS - 1:TAIL_ROWS, :]
        p2 = prev[TAIL_ROWS - 2:TAIL_ROWS - 1, :]
        u1 = jnp.where(row == 0, p1, pltpu.roll(u, 1, 0))
        u2 = jnp.where(row == 0, p2, jnp.where(row == 1, p1, pltpu.roll(u, 2, 0)))
        cw = cw_ref[:, lo:lo + FF_CHUNK]
        return cw[0:1] * u2 + cw[1:2] * u1 + cw[2:3] * u + cb_ref[:, lo:lo + FF_CHUNK]

    for c in range(d_ff // FF_CHUNK):
        a = conv_chunk(c * FF_CHUNK)
        bgate = conv_chunk(d_ff + c * FF_CHUNK)
        act_ref[:, c * FF_CHUNK:(c + 1) * FF_CHUNK] = (a * jax.nn.sigmoid(a) * bgate).astype(BF16)

    y = x + _dot(act_ref[...], wdown_ref[...])
    if final_norm:
        y = _rms_normalize(y, gfin_ref[...])
    o_ref[...] = y


def _ffn(x2, g, wup, cw, cb, wdown, gfin, *, seq_len, final_norm):
    n, d = x2.shape
    d_ff = wdown.shape[0]
    tm = ROW_TILE
    row = pl.BlockSpec((tm, d), lambda i: (i, 0))
    full = lambda a: pl.BlockSpec(a.shape, lambda i: (0, 0))
    return pl.pallas_call(
        functools.partial(_ffn_kernel, tiles_per_seq=seq_len // tm, final_norm=final_norm),
        grid=(n // tm,),
        in_specs=[row, full(g), full(wup), full(cw), full(cb), full(wdown), full(gfin)],
        out_specs=row,
        out_shape=jax.ShapeDtypeStruct((n, d), F32),
        scratch_shapes=[pltpu.VMEM((TAIL_ROWS, 2 * d_ff), F32), pltpu.VMEM((tm, d_ff), BF16)],
        compiler_params=pltpu.CompilerParams(
            dimension_semantics=("arbitrary",), vmem_limit_bytes=VMEM_LIMIT),
        name="conv_ffn",
    )(x2, g, wup, cw, cb, wdown, gfin)


def kernel(x, g_mix, w_in, w_proj_moba, w_proj_sb, w_out, g_ffn, w_up, conv_w, conv_b, w_down, g_final):
    b, t, d = x.shape
    depth = w_in.shape[0]
    assert t % ROW_TILE == 0 and ROW_TILE % KEY_BLOCK == 0 and Q_TILE == KEY_BLOCK
    assert w_up.shape[2] % (2 * FF_CHUNK) == 0 and conv_w.shape[1] == CONV_WIDTH
    slopes = jnp.exp2(-8.0 * jnp.arange(1, N_HEADS + 1, dtype=F32) / N_HEADS)
    gfin = g_final.reshape(1, d)
    for layer in range(depth):
        qta, ka, vta, qtb, kb, vtb, gate = _inproj(x, g_mix[layer], w_in[layer].astype(BF16))
        ya = _moba_attention(slopes, qta, ka, vta)
        yb = _sb_attention(qtb, kb, vtb)
        x1 = _mix(x.reshape(b * t, d), ya.reshape(b * t, W_ATT), yb.reshape(b * t, W_ATT),
                  gate.reshape(b * t, 2 * d), w_proj_moba[layer].astype(BF16),
                  w_proj_sb[layer].astype(BF16), w_out[layer].astype(BF16))
        x = _ffn(x1, g_ffn[layer].reshape(1, d), w_up[layer].astype(BF16),
                 conv_w[layer].reshape(CONV_WIDTH, -1), conv_b[layer].reshape(1, -1),
                 w_down[layer].astype(BF16), gfin, seq_len=t,
                 final_norm=(layer == depth - 1)).reshape(b, t, d)
    return x
```

```python
import functools

import jax
import jax.numpy as jnp
from jax import lax
from jax.experimental import pallas as pl
from jax.experimental.pallas import tpu as pltpu

F32 = jnp.float32
BF16 = jnp.bfloat16

HEAD_DIM = 64
N_HEADS = 8
W_ATT = N_HEADS * HEAD_DIM
HEADS_PER_STEP = 2
LANES = 128
KEY_BLOCK = 256
Q_TILE = 256
MOBA_TOPK = 3
MOBA_UNROLL = 2
LOG2E = 1.4426950408889634
RMS_EPS = 1e-6
CONV_WIDTH = 3
FF_CHUNK = 256
ROW_TILE = 512
TAIL_ROWS = 8
SB_LOG2_WEIGHT_FLOOR = 160.0
VMEM_LIMIT = 56 * 1024 * 1024


def _dot(a, b):
    return jnp.dot(a, b, preferred_element_type=F32)


def _rms_normalize(x, g):
    return x * lax.rsqrt(jnp.mean(x * x, axis=-1, keepdims=True) + RMS_EPS) * g


def _inproj_kernel(x_ref, g_ref, w_ref, qa_ref, ka_ref, va_ref, qb_ref, kb_ref, vb_ref, gate_ref):
    h = _rms_normalize(x_ref[0], g_ref[...]).astype(BF16)
    scale = HEAD_DIM ** -0.5

    def proj(lo, hi):
        return _dot(h, w_ref[:, lo:hi])

    def put_blocks(ref, t):
        for j in range(t.shape[1] // KEY_BLOCK):
            ref[0, j] = t[:, j * KEY_BLOCK:(j + 1) * KEY_BLOCK].astype(BF16)

    w = W_ATT
    qa_ref[0] = (proj(0, w) * (scale * LOG2E)).T.astype(BF16)
    ka_ref[0] = proj(w, 2 * w).astype(BF16)
    put_blocks(va_ref, proj(2 * w, 3 * w).T)
    qb_ref[0] = (proj(3 * w, 4 * w) * (scale * LOG2E)).T.astype(BF16)
    kb_ref[0] = proj(4 * w, 5 * w).astype(BF16)
    put_blocks(vb_ref, proj(5 * w, 6 * w).T)
    gate_ref[0] = jax.nn.sigmoid(proj(6 * w, w_ref.shape[1])).astype(BF16)


def _inproj(x, g, w_in):
    b, t, d = x.shape
    d_in = w_in.shape[1]
    tm = ROW_TILE
    nblk = t // KEY_BLOCK
    qt_shape = jax.ShapeDtypeStruct((b, W_ATT, t), BF16)
    k_shape = jax.ShapeDtypeStruct((b, t, W_ATT), BF16)
    vt_shape = jax.ShapeDtypeStruct((b, nblk, W_ATT, KEY_BLOCK), BF16)
    gate_shape = jax.ShapeDtypeStruct((b, t, d_in - 6 * W_ATT), BF16)
    qt_spec = pl.BlockSpec((1, W_ATT, tm), lambda bi, i: (bi, 0, i))
    k_spec = pl.BlockSpec((1, tm, W_ATT), lambda bi, i: (bi, i, 0))
    vt_spec = pl.BlockSpec((1, tm // KEY_BLOCK, W_ATT, KEY_BLOCK), lambda bi, i: (bi, i, 0, 0))
    return pl.pallas_call(
        _inproj_kernel,
        grid=(b, t // tm),
        in_specs=[
            pl.BlockSpec((1, tm, d), lambda bi, i: (bi, i, 0)),
            pl.BlockSpec((1, d), lambda bi, i: (0, 0)),
            pl.BlockSpec((d, d_in), lambda bi, i: (0, 0)),
        ],
        out_specs=[qt_spec, k_spec, vt_spec, qt_spec, k_spec, vt_spec,
                   pl.BlockSpec((1, tm, d_in - 6 * W_ATT), lambda bi, i: (bi, i, 0))],
        out_shape=[qt_shape, k_shape, vt_shape, qt_shape, k_shape, vt_shape, gate_shape],
        compiler_params=pltpu.CompilerParams(
            dimension_semantics=("parallel", "parallel"), vmem_limit_bytes=VMEM_LIMIT),
        name="inproj",
    )(x, g.reshape(1, d), w_in)


def _head_queries(qt, h):
    row = lax.broadcasted_iota(jnp.int32, qt.shape, 0)
    keep = (row >= h * HEAD_DIM) & (row < (h + 1) * HEAD_DIM)
    return jnp.where(keep, qt.astype(F32), 0.0).astype(BF16)


def _key_block(k_ref, j):
    return k_ref[0, pl.ds(pl.multiple_of(j * KEY_BLOCK, KEY_BLOCK), KEY_BLOCK), :]


def _value_block(vt_ref, j, h):
    return vt_ref[0, j, h * HEAD_DIM:(h + 1) * HEAD_DIM, :]


def _moba_kernel(slopes_ref, qt_ref, k_ref, vt_ref, y_ref, kmean_ref, rowbias_ref, tile_ref, s_ref, *, nblk):
    hp = pl.program_id(1)
    qi = pl.program_id(2)
    tq = Q_TILE

    @pl.when(qi == 0)
    def _():
        def body(j, c):
            blk = _key_block(k_ref, j).astype(F32)
            kmean_ref[pl.ds(j, 1), :] = jnp.sum(blk, axis=0, keepdims=True) * (1.0 / KEY_BLOCK)
            return c
        lax.fori_loop(0, nblk, body, 0)

    km = kmean_ref[...]
    km_hi = km.astype(BF16)
    km_r = km - km_hi.astype(F32)
    km_mid = km_r.astype(BF16)
    km_lo = (km_r - km_mid.astype(F32)).astype(BF16)

    qt = qt_ref[0]
    blk_id = lax.broadcasted_iota(jnp.int32, (nblk, tq), 0).astype(F32)
    key_in_blk = lax.broadcasted_iota(jnp.int32, (KEY_BLOCK, tq), 0)
    qry_in_blk = lax.broadcasted_iota(jnp.int32, (KEY_BLOCK, tq), 1)
    rel = (key_in_blk - qry_in_blk).astype(F32)
    own = qi.astype(F32)
    neg_inf = -jnp.inf

    heads = range(HEADS_PER_STEP)
    qhs = []
    for h in heads:
        slope2 = slopes_ref[hp * HEADS_PER_STEP + h] * LOG2E
        qh = _head_queries(qt, h)
        gate = _dot(km_hi, qh) + _dot(km_mid, qh) + _dot(km_lo, qh)
        gate = jnp.where(blk_id < own, gate, neg_inf)
        chosen = jnp.where(blk_id == own, 1.0, 0.0)
        for _ in range(min(MOBA_TOPK, nblk)):
            best = jnp.max(gate, axis=0, keepdims=True)
            first = jnp.min(jnp.where(gate == best, blk_id, float(nblk)), axis=0, keepdims=True)
            hit = blk_id == first
            chosen = jnp.where(hit & (best > neg_inf), 1.0, chosen)
            gate = jnp.where(hit, neg_inf, gate)
        rowbias_ref[h, :nblk] = jnp.where(chosen > 0.0, slope2 * KEY_BLOCK * (blk_id - own), neg_inf)
        rowbias_ref[h, nblk:] = jnp.full((TAIL_ROWS, tq), neg_inf, F32)
        alibi = slope2 * rel
        tile_ref[h, 0] = alibi
        tile_ref[h, 1] = jnp.where(key_in_blk <= qry_in_blk, alibi, neg_inf)
        qhs.append(qh)

    def position(t):
        t = jnp.asarray(t, jnp.int32)
        is_own = t == 0
        kidx = jnp.where(is_own, qi, jnp.minimum(t - 1, jnp.maximum(qi - 1, 0)))
        ridx = jnp.where(is_own, qi, jnp.where(t <= qi, t - 1, nblk))
        return is_own.astype(jnp.int32), kidx, ridx

    def score_stage(i, slot):
        maxes = []
        for g in range(MOBA_UNROLL):
            kind, kidx, ridx = position(i * MOBA_UNROLL + g)
            kblk = _key_block(k_ref, kidx)
            for h in heads:
                s = _dot(kblk, qhs[h]) + tile_ref[h, kind]
                s_ref[slot, g * HEADS_PER_STEP + h] = s
                maxes.append(jnp.max(s, axis=0, keepdims=True) + rowbias_ref[h, pl.ds(ridx, 1), :])
        return tuple(maxes)

    def value_stage(i, slot, maxes, states):
        new_states = []
        for h in heads:
            m, l, acc = states[h]
            m_new = m
            for g in range(MOBA_UNROLL):
                m_new = jnp.maximum(m_new, maxes[g * HEADS_PER_STEP + h])
            alpha = jnp.exp2(m - m_new)
            l = alpha * l
            acc = alpha * acc
            for g in range(MOBA_UNROLL):
                _, kidx, ridx = position(i * MOBA_UNROLL + g)
                shift = m_new - rowbias_ref[h, pl.ds(ridx, 1), :]
                p = jnp.exp2(s_ref[slot, g * HEADS_PER_STEP + h] - shift)
                l = l + jnp.sum(p, axis=0, keepdims=True)
                acc = acc + _dot(_value_block(vt_ref, kidx, h), p.astype(BF16))
            new_states.append((m_new, l, acc))
        return tuple(new_states)

    def body(ii, carry):
        maxes0, states = carry
        i = 2 * ii
        maxes1 = score_stage(i + 1, 1)
        states = value_stage(i, 0, maxes0, states)
        maxes2 = score_stage(i + 2, 0)
        states = value_stage(i + 1, 1, maxes1, states)
        return maxes2, states

    n_steps = (qi + MOBA_UNROLL) // MOBA_UNROLL
    n_pairs = (n_steps - 1) // 2
    init = tuple((jnp.full((1, tq), neg_inf, F32), jnp.zeros((1, tq), F32), jnp.zeros((HEAD_DIM, tq), F32))
                 for _ in heads)
    maxes0, states = lax.fori_loop(0, n_pairs, body, (score_stage(0, 0), init))
    last = 2 * n_pairs
    maxes1 = score_stage(last + 1, 1)
    states = value_stage(last, 0, maxes0, states)
    states = value_stage(last + 1, 1, maxes1, states)
    outs = [acc / l for _, l, acc in states]

    y_ref[0] = jnp.concatenate(outs, axis=0).T.astype(BF16)


def _attention_specs(b, t):
    nblk = t // KEY_BLOCK
    width = HEADS_PER_STEP * HEAD_DIM
    in_specs = [
        pl.BlockSpec((1, width, Q_TILE), lambda bi, hp, qi: (bi, hp, qi)),
        pl.BlockSpec((1, t, width), lambda bi, hp, qi: (bi, 0, hp)),
        pl.BlockSpec((1, nblk, width, KEY_BLOCK), lambda bi, hp, qi: (bi, 0, hp, 0)),
    ]
    out_spec = pl.BlockSpec((1, Q_TILE, width), lambda bi, hp, qi: (bi, qi, hp))
    grid = (b, N_HEADS // HEADS_PER_STEP, t // Q_TILE)
    return grid, in_specs, out_spec


def _moba_attention(slopes, qt, k, vt):
    b, t, _ = k.shape
    nblk = t // KEY_BLOCK
    grid, in_specs, out_spec = _attention_specs(b, t)
    return pl.pallas_call(
        functools.partial(_moba_kernel, nblk=nblk),
        grid=grid,
        in_specs=[pl.BlockSpec(memory_space=pltpu.SMEM)] + in_specs,
        out_specs=out_spec,
        out_shape=jax.ShapeDtypeStruct((b, t, W_ATT), BF16),
        scratch_shapes=[pltpu.VMEM((nblk, HEADS_PER_STEP * HEAD_DIM), F32),
                        pltpu.VMEM((HEADS_PER_STEP, nblk + TAIL_ROWS, Q_TILE), F32),
                        pltpu.VMEM((HEADS_PER_STEP, 2, KEY_BLOCK, Q_TILE), F32),
                        pltpu.VMEM((2, MOBA_UNROLL * HEADS_PER_STEP, KEY_BLOCK, Q_TILE), F32)],
        compiler_params=pltpu.CompilerParams(
            dimension_semantics=("parallel", "parallel", "arbitrary"), vmem_limit_bytes=VMEM_LIMIT),
        name="moba_attention",
    )(slopes, qt, k, vt)


def _softplus2(z2):
    return jnp.maximum(z2, 0.0) + jnp.log2(1.0 + jnp.exp2(-jnp.abs(z2)))


def _suffix_sums(later_keys, x):
    hi = x.astype(BF16)
    lo = (x - hi.astype(F32)).astype(BF16)
    return _dot(later_keys, hi) + _dot(later_keys, lo)


def _sb_kernel(qt_ref, k_ref, vt_ref, y_ref):
    qi = pl.program_id(2)
    tq = Q_TILE
    qt = qt_ref[0]
    key_in_blk = lax.broadcasted_iota(jnp.int32, (KEY_BLOCK, tq), 0)
    qry_in_blk = lax.broadcasted_iota(jnp.int32, (KEY_BLOCK, tq), 1)
    strict = key_in_blk < qry_in_blk
    r = lax.broadcasted_iota(jnp.int32, (KEY_BLOCK, KEY_BLOCK), 0)
    c = lax.broadcasted_iota(jnp.int32, (KEY_BLOCK, KEY_BLOCK), 1)
    later_keys = jnp.where(c > r, 1.0, 0.0).astype(BF16)

    heads = range(HEADS_PER_STEP)
    qhs = [_head_queries(qt, h) for h in heads]

    has_prev = qi > 0
    prev = jnp.maximum(qi - 1, 0)
    z_own = [_dot(_key_block(k_ref, qi), qhs[h]) for h in heads]
    z_prev = [_dot(_key_block(k_ref, prev), qhs[h]) for h in heads]
    sp_own = [jnp.where(strict, _softplus2(z_own[h]), 0.0) for h in heads]
    sp_prev = [_softplus2(z_prev[h]) for h in heads]
    tail_own = [_suffix_sums(later_keys, sp_own[h]) for h in heads]
    tail_prev = [_suffix_sums(later_keys, sp_prev[h]) for h in heads]
    accs, carries = [], []
    for h in heads:
        w_own = jnp.where(strict, jnp.exp2(z_own[h] - sp_own[h] - tail_own[h]), 0.0)
        carry = jnp.where(has_prev, tail_own[h][0:1, :] + sp_own[h][0:1, :], jnp.inf)
        w_prev = jnp.exp2(z_prev[h] - sp_prev[h] - tail_prev[h] - carry)
        accs.append(_dot(_value_block(vt_ref, qi, h), w_own.astype(BF16))
                    + _dot(_value_block(vt_ref, prev, h), w_prev.astype(BF16)))
        carries.append(carry + tail_prev[h][0:1, :] + sp_prev[h][0:1, :])

    def carry_min(carries):
        return jnp.min(jnp.minimum(carries[0], carries[1]))

    def cond(state):
        j, cmin, _, _ = state
        return (j >= 0) & (cmin < SB_LOG2_WEIGHT_FLOOR)

    def body(state):
        j, _, carries, accs = state
        new_carries, new_accs = [], []
        for h in heads:
            z2 = _dot(_key_block(k_ref, j), qhs[h])
            sp = _softplus2(z2)
            tail = _suffix_sums(later_keys, sp) + carries[h]
            w = jnp.exp2(z2 - sp - tail)
            new_accs.append(accs[h] + _dot(_value_block(vt_ref, j, h), w.astype(BF16)))
            new_carries.append(tail[0:1, :] + sp[0:1, :])
        return j - 1, carry_min(new_carries), tuple(new_carries), tuple(new_accs)

    _, _, _, accs = lax.while_loop(cond, body, (qi - 2, carry_min(carries), tuple(carries), tuple(accs)))

    y_ref[0] = jnp.concatenate(accs, axis=0).T.astype(BF16)


def _sb_attention(qt, k, vt):
    b, t, _ = k.shape
    grid, in_specs, out_spec = _attention_specs(b, t)
    return pl.pallas_call(
        _sb_kernel,
        grid=grid,
        in_specs=in_specs,
        out_specs=out_spec,
        out_shape=jax.ShapeDtypeStruct((b, t, W_ATT), BF16),
        compiler_params=pltpu.CompilerParams(
            dimension_semantics=("parallel", "parallel", "arbitrary"), vmem_limit_bytes=VMEM_LIMIT),
        name="stickbreak_attention",
    )(qt, k, vt)


def _mix_kernel(x_ref, ya_ref, yb_ref, gate_ref, wa_ref, wb_ref, wo_ref, o_ref):
    d = x_ref.shape[1]
    ga = gate_ref[:, :d].astype(F32)
    gb = gate_ref[:, d:].astype(F32)
    mixed = ga * _dot(ya_ref[...], wa_ref[...]) + gb * _dot(yb_ref[...], wb_ref[...])
    o_ref[...] = x_ref[...] + _dot(mixed.astype(BF16), wo_ref[...])


def _mix(x2, ya, yb, gate, wa, wb, wo):
    n, d = x2.shape
    tm = ROW_TILE
    row = lambda width: pl.BlockSpec((tm, width), lambda i: (i, 0))
    full = lambda a: pl.BlockSpec(a.shape, lambda i: (0, 0))
    return pl.pallas_call(
        _mix_kernel,
        grid=(n // tm,),
        in_specs=[row(d), row(W_ATT), row(W_ATT), row(2 * d), full(wa), full(wb), full(wo)],
        out_specs=row(d),
        out_shape=jax.ShapeDtypeStruct((n, d), F32),
        compiler_params=pltpu.CompilerParams(
            dimension_semantics=("parallel",), vmem_limit_bytes=VMEM_LIMIT),
        name="mix_outproj",
    )(x2, ya, yb, gate, wa, wb, wo)


def _ffn_kernel(x_ref, g_ref, wup_ref, cw_ref, cb_ref, wdown_ref, gfin_ref, o_ref, tail_ref, act_ref,
                *, tiles_per_seq, final_norm):
    tm = x_ref.shape[0]
    d_ff = wdown_ref.shape[0]

    @pl.when(pl.program_id(0) % tiles_per_seq == 0)
    def _():
        tail_ref[...] = jnp.zeros_like(tail_ref)

    x = x_ref[...]
    h = _rms_normalize(x, g_ref[...]).astype(BF16)
    row = lax.broadcasted_iota(jnp.int32, (tm, FF_CHUNK), 0)

    def conv_chunk(lo):
        u = _dot(h, wup_ref[:, lo:lo + FF_CHUNK])
        prev = tail_ref[:, lo:lo + FF_CHUNK]
        tail_ref[:, lo:lo + FF_CHUNK] = u[tm - TAIL_ROWS:, :]
        p1 = prev[TAIL_ROWS - 1:TAIL_ROWS, :]
        p2 = prev[TAIL_ROWS - 2:TAIL_ROWS - 1, :]
        u1 = jnp.where(row == 0, p1, pltpu.roll(u, 1, 0))
        u2 = jnp.where(row == 0, p2, jnp.where(row == 1, p1, pltpu.roll(u, 2, 0)))
        cw = cw_ref[:, lo:lo + FF_CHUNK]
        return cw[0:1] * u2 + cw[1:2] * u1 + cw[2:3] * u + cb_ref[:, lo:lo + FF_CHUNK]

    for c in range(d_ff // FF_CHUNK):
        a = conv_chunk(c * FF_CHUNK)
        bgate = conv_chunk(d_ff + c * FF_CHUNK)
        act_ref[:, c * FF_CHUNK:(c + 1) * FF_CHUNK] = (a * jax.nn.sigmoid(a) * bgate).astype(BF16)

    y = x + _dot(act_ref[...], wdown_ref[...])
    if final_norm:
        y = _rms_normalize(y, gfin_ref[...])
    o_ref[...] = y


def _ffn(x2, g, wup, cw, cb, wdown, gfin, *, seq_len, final_norm):
    n, d = x2.shape
    d_ff = wdown.shape[0]
    tm = ROW_TILE
    row = pl.BlockSpec((tm, d), lambda i: (i, 0))
    full = lambda a: pl.BlockSpec(a.shape, lambda i: (0, 0))
    return pl.pallas_call(
        functools.partial(_ffn_kernel, tiles_per_seq=seq_len // tm, final_norm=final_norm),
        grid=(n // tm,),
        in_specs=[row, full(g), full(wup), full(cw), full(cb), full(wdown), full(gfin)],
        out_specs=row,
        out_shape=jax.ShapeDtypeStruct((n, d), F32),
        scratch_shapes=[pltpu.VMEM((TAIL_ROWS, 2 * d_ff), F32), pltpu.VMEM((tm, d_ff), BF16)],
        compiler_params=pltpu.CompilerParams(
            dimension_semantics=("arbitrary",), vmem_limit_bytes=VMEM_LIMIT),
        name="conv_ffn",
    )(x2, g, wup, cw, cb, wdown, gfin)


def kernel(x, g_mix, w_in, w_proj_moba, w_proj_sb, w_out, g_ffn, w_up, conv_w, conv_b, w_down, g_final):
    b, t, d = x.shape
    depth = w_in.shape[0]
    assert t % ROW_TILE == 0 and ROW_TILE % KEY_BLOCK == 0 and Q_TILE == KEY_BLOCK
    assert w_up.shape[2] % (2 * FF_CHUNK) == 0 and conv_w.shape[1] == CONV_WIDTH
    slopes = jnp.exp2(-8.0 * jnp.arange(1, N_HEADS + 1, dtype=F32) / N_HEADS)
    gfin = g_final.reshape(1, d)
    for layer in range(depth):
        qta, ka, vta, qtb, kb, vtb, gate = _inproj(x, g_mix[layer], w_in[layer].astype(BF16))
        ya = _moba_attention(slopes, qta, ka, vta)
        yb = _sb_attention(qtb, kb, vtb)
        x1 = _mix(x.reshape(b * t, d), ya.reshape(b * t, W_ATT), yb.reshape(b * t, W_ATT),
                  gate.reshape(b * t, 2 * d), w_proj_moba[layer].astype(BF16),
                  w_proj_sb[layer].astype(BF16), w_out[layer].astype(BF16))
        x = _ffn(x1, g_ffn[layer].reshape(1, d), w_up[layer].astype(BF16),
                 conv_w[layer].reshape(CONV_WIDTH, -1), conv_b[layer].reshape(1, -1),
                 w_down[layer].astype(BF16), gfin, seq_len=t,
                 final_norm=(layer == depth - 1)).reshape(b, t, d)
    return x
```

```python
import functools

import jax
import jax.numpy as jnp
from jax import lax
from jax.experimental import pallas as pl
from jax.experimental.pallas import tpu as pltpu

F32 = jnp.float32
BF16 = jnp.bfloat16

HEAD_DIM = 64
N_HEADS = 8
W_ATT = N_HEADS * HEAD_DIM
HEADS_PER_STEP = 2
LANES = 128
KEY_BLOCK = 256
Q_TILE = 256
MOBA_TOPK = 3
MOBA_UNROLL = 2
LOG2E = 1.4426950408889634
RMS_EPS = 1e-6
CONV_WIDTH = 3
FF_CHUNK = 256
ROW_TILE = 512
TAIL_ROWS = 8
SB_LOG2_WEIGHT_FLOOR = 160.0
VMEM_LIMIT = 56 * 1024 * 1024


def _dot(a, b):
    return jnp.dot(a, b, preferred_element_type=F32)


def _rms_normalize(x, g):
    return x * lax.rsqrt(jnp.mean(x * x, axis=-1, keepdims=True) + RMS_EPS) * g


def _inproj_kernel(x_ref, g_ref, w_ref, qa_ref, ka_ref, va_ref, qb_ref, kb_ref, vb_ref, gate_ref):
    h = _rms_normalize(x_ref[0], g_ref[...]).astype(BF16)
    scale = HEAD_DIM ** -0.5

    def proj(lo, hi):
        return _dot(h, w_ref[:, lo:hi])

    def put_blocks(ref, t):
        for j in range(t.shape[1] // KEY_BLOCK):
            ref[0, j] = t[:, j * KEY_BLOCK:(j + 1) * KEY_BLOCK].astype(BF16)

    w = W_ATT
    qa_ref[0] = (proj(0, w) * (scale * LOG2E)).T.astype(BF16)
    ka_ref[0] = proj(w, 2 * w).astype(BF16)
    put_blocks(va_ref, proj(2 * w, 3 * w).T)
    qb_ref[0] = (proj(3 * w, 4 * w) * (scale * LOG2E)).T.astype(BF16)
    kb_ref[0] = proj(4 * w, 5 * w).astype(BF16)
    put_blocks(vb_ref, proj(5 * w, 6 * w).T)
    gate_ref[0] = jax.nn.sigmoid(proj(6 * w, w_ref.shape[1])).astype(BF16)


def _inproj(x, g, w_in):
    b, t, d = x.shape
    d_in = w_in.shape[1]
    tm = ROW_TILE
    nblk = t // KEY_BLOCK
    qt_shape = jax.ShapeDtypeStruct((b, W_ATT, t), BF16)
    k_shape = jax.ShapeDtypeStruct((b, t, W_ATT), BF16)
    vt_shape = jax.ShapeDtypeStruct((b, nblk, W_ATT, KEY_BLOCK), BF16)
    gate_shape = jax.ShapeDtypeStruct((b, t, d_in - 6 * W_ATT), BF16)
    qt_spec = pl.BlockSpec((1, W_ATT, tm), lambda bi, i: (bi, 0, i))
    k_spec = pl.BlockSpec((1, tm, W_ATT), lambda bi, i: (bi, i, 0))
    vt_spec = pl.BlockSpec((1, tm // KEY_BLOCK, W_ATT, KEY_BLOCK), lambda bi, i: (bi, i, 0, 0))
    return pl.pallas_call(
        _inproj_kernel,
        grid=(b, t // tm),
        in_specs=[
            pl.BlockSpec((1, tm, d), lambda bi, i: (bi, i, 0)),
            pl.BlockSpec((1, d), lambda bi, i: (0, 0)),
            pl.BlockSpec((d, d_in), lambda bi, i: (0, 0)),
        ],
        out_specs=[qt_spec, k_spec, vt_spec, qt_spec, k_spec, vt_spec,
                   pl.BlockSpec((1, tm, d_in - 6 * W_ATT), lambda bi, i: (bi, i, 0))],
        out_shape=[qt_shape, k_shape, vt_shape, qt_shape, k_shape, vt_shape, gate_shape],
        compiler_params=pltpu.CompilerParams(
            dimension_semantics=("parallel", "parallel"), vmem_limit_bytes=VMEM_LIMIT),
        name="inproj",
    )(x, g.reshape(1, d), w_in)


def _head_queries(qt, h):
    row = lax.broadcasted_iota(jnp.int32, qt.shape, 0)
    keep = (row >= h * HEAD_DIM) & (row < (h + 1) * HEAD_DIM)
    return jnp.where(keep, qt.astype(F32), 0.0).astype(BF16)


def _key_block(k_ref, j):
    return k_ref[0, pl.ds(pl.multiple_of(j * KEY_BLOCK, KEY_BLOCK), KEY_BLOCK), :]


def _value_block(vt_ref, j, h):
    return vt_ref[0, j, h * HEAD_DIM:(h + 1) * HEAD_DIM, :]


def _moba_kernel(slopes_ref, qt_ref, k_ref, vt_ref, y_ref, kmean_ref, rowbias_ref, tile_ref, s_ref, *, nblk):
    hp = pl.program_id(1)
    qi = pl.program_id(2)
    tq = Q_TILE

    @pl.when(qi == 0)
    def _():
        def body(j, c):
            blk = _key_block(k_ref, j).astype(F32)
            kmean_ref[pl.ds(j, 1), :] = jnp.sum(blk, axis=0, keepdims=True) * (1.0 / KEY_BLOCK)
            return c
        lax.fori_loop(0, nblk, body, 0)

    km = kmean_ref[...]
    km_hi = km.astype(BF16)
    km_r = km - km_hi.astype(F32)
    km_mid = km_r.astype(BF16)
    km_lo = (km_r - km_mid.astype(F32)).astype(BF16)

    qt = qt_ref[0]
    blk_id = lax.broadcasted_iota(jnp.int32, (nblk, tq), 0).astype(F32)
    key_in_blk = lax.broadcasted_iota(jnp.int32, (KEY_BLOCK, tq), 0)
    qry_in_blk = lax.broadcasted_iota(jnp.int32, (KEY_BLOCK, tq), 1)
    rel = (key_in_blk - qry_in_blk).astype(F32)
    own = qi.astype(F32)
    neg_inf = -jnp.inf

    heads = range(HEADS_PER_STEP)
    qhs = []
    for h in heads:
        slope2 = slopes_ref[hp * HEADS_PER_STEP + h] * LOG2E
        qh = _head_queries(qt, h)
        gate = _dot(km_hi, qh) + _dot(km_mid, qh) + _dot(km_lo, qh)
        gate = jnp.where(blk_id < own, gate, neg_inf)
        chosen = jnp.where(blk_id == own, 1.0, 0.0)
        for _ in range(min(MOBA_TOPK, nblk)):
            best = jnp.max(gate, axis=0, keepdims=True)
            first = jnp.min(jnp.where(gate == best, blk_id, float(nblk)), axis=0, keepdims=True)
            hit = blk_id == first
            chosen = jnp.where(hit & (best > neg_inf), 1.0, chosen)
            gate = jnp.where(hit, neg_inf, gate)
        rowbias_ref[h, :nblk] = jnp.where(chosen > 0.0, slope2 * KEY_BLOCK * (blk_id - own), neg_inf)
        rowbias_ref[h, nblk:] = jnp.full((TAIL_ROWS, tq), neg_inf, F32)
        alibi = slope2 * rel
        tile_ref[h, 0] = alibi
        tile_ref[h, 1] = jnp.where(key_in_blk <= qry_in_blk, alibi, neg_inf)
        qhs.append(qh)

    def position(t):
        t = jnp.asarray(t, jnp.int32)
        is_own = t == 0
        kidx = jnp.where(is_own, qi, jnp.minimum(t - 1, jnp.maximum(qi - 1, 0)))
        ridx = jnp.where(is_own, qi, jnp.where(t <= qi, t - 1, nblk))
        return is_own.astype(jnp.int32), kidx, ridx

    def score_stage(i, slot):
        maxes = []
        for g in range(MOBA_UNROLL):
            kind, kidx, ridx = position(i * MOBA_UNROLL + g)
            kblk = _key_block(k_ref, kidx)
            for h in heads:
                s = _dot(kblk, qhs[h]) + tile_ref[h, kind]
                s_ref[slot, g * HEADS_PER_STEP + h] = s
                maxes.append(jnp.max(s, axis=0, keepdims=True) + rowbias_ref[h, pl.ds(ridx, 1), :])
        return tuple(maxes)

    def value_stage(i, slot, maxes, states):
        new_states = []
        for h in heads:
            m, l, acc = states[h]
            m_new = m
            for g in range(MOBA_UNROLL):
                m_new = jnp.maximum(m_new, maxes[g * HEADS_PER_STEP + h])
            alpha = jnp.exp2(m - m_new)
            l = alpha * l
            acc = alpha * acc
            for g in range(MOBA_UNROLL):
                _, kidx, ridx = position(i * MOBA_UNROLL + g)
                shift = m_new - rowbias_ref[h, pl.ds(ridx, 1), :]
                p = jnp.exp2(s_ref[slot, g * HEADS_PER_STEP + h] - shift)
                l = l + jnp.sum(p, axis=0, keepdims=True)
                acc = acc + _dot(_value_block(vt_ref, kidx, h), p.astype(BF16))
            new_states.append((m_new, l, acc))
        return tuple(new_states)

    def body(ii, carry):
        maxes0, states = carry
        i = 2 * ii
        maxes1 = score_stage(i + 1, 1)
        states = value_stage(i, 0, maxes0, states)
        maxes2 = score_stage(i + 2, 0)
        states = value_stage(i + 1, 1, maxes1, states)
        return maxes2, states

    n_steps = (qi + MOBA_UNROLL) // MOBA_UNROLL
    n_pairs = (n_steps - 1) // 2
    init = tuple((jnp.full((1, tq), neg_inf, F32), jnp.zeros((1, tq), F32), jnp.zeros((HEAD_DIM, tq), F32))
                 for _ in heads)
    maxes0, states = lax.fori_loop(0, n_pairs, body, (score_stage(0, 0), init))
    last = 2 * n_pairs

    def two_steps(maxes0, states):
        maxes1 = score_stage(last + 1, 1)
        states = value_stage(last, 0, maxes0, states)
        return value_stage(last + 1, 1, maxes1, states)

    def one_step(maxes0, states):
        return value_stage(last, 0, maxes0, states)

    states = lax.cond(last + 1 < n_steps, two_steps, one_step, maxes0, states)
    outs = [acc / l for _, l, acc in states]

    y_ref[0] = jnp.concatenate(outs, axis=0).T.astype(BF16)


def _attention_specs(b, t):
    nblk = t // KEY_BLOCK
    width = HEADS_PER_STEP * HEAD_DIM
    in_specs = [
        pl.BlockSpec((1, width, Q_TILE), lambda bi, hp, qi: (bi, hp, qi)),
        pl.BlockSpec((1, t, width), lambda bi, hp, qi: (bi, 0, hp)),
        pl.BlockSpec((1, nblk, width, KEY_BLOCK), lambda bi, hp, qi: (bi, 0, hp, 0)),
    ]
    out_spec = pl.BlockSpec((1, Q_TILE, width), lambda bi, hp, qi: (bi, qi, hp))
    grid = (b, N_HEADS // HEADS_PER_STEP, t // Q_TILE)
    return grid, in_specs, out_spec


def _moba_attention(slopes, qt, k, vt):
    b, t, _ = k.shape
    nblk = t // KEY_BLOCK
    grid, in_specs, out_spec = _attention_specs(b, t)
    return pl.pallas_call(
        functools.partial(_moba_kernel, nblk=nblk),
        grid=grid,
        in_specs=[pl.BlockSpec(memory_space=pltpu.SMEM)] + in_specs,
        out_specs=out_spec,
        out_shape=jax.ShapeDtypeStruct((b, t, W_ATT), BF16),
        scratch_shapes=[pltpu.VMEM((nblk, HEADS_PER_STEP * HEAD_DIM), F32),
                        pltpu.VMEM((HEADS_PER_STEP, nblk + TAIL_ROWS, Q_TILE), F32),
                        pltpu.VMEM((HEADS_PER_STEP, 2, KEY_BLOCK, Q_TILE), F32),
                        pltpu.VMEM((2, MOBA_UNROLL * HEADS_PER_STEP, KEY_BLOCK, Q_TILE), F32)],
        compiler_params=pltpu.CompilerParams(
            dimension_semantics=("parallel", "parallel", "arbitrary"), vmem_limit_bytes=VMEM_LIMIT),
        name="moba_attention",
    )(slopes, qt, k, vt)


def _softplus2(z2):
    return jnp.maximum(z2, 0.0) + jnp.log2(1.0 + jnp.exp2(-jnp.abs(z2)))


def _suffix_sums(later_keys, x):
    return _dot(later_keys, x.astype(BF16))


def _sb_kernel(qt_ref, k_ref, vt_ref, y_ref):
    qi = pl.program_id(2)
    tq = Q_TILE
    qt = qt_ref[0]
    key_in_blk = lax.broadcasted_iota(jnp.int32, (KEY_BLOCK, tq), 0)
    qry_in_blk = lax.broadcasted_iota(jnp.int32, (KEY_BLOCK, tq), 1)
    strict = key_in_blk < qry_in_blk
    r = lax.broadcasted_iota(jnp.int32, (KEY_BLOCK, KEY_BLOCK), 0)
    c = lax.broadcasted_iota(jnp.int32, (KEY_BLOCK, KEY_BLOCK), 1)
    later_keys = jnp.where(c > r, 1.0, 0.0).astype(BF16)

    heads = range(HEADS_PER_STEP)
    qhs = [_head_queries(qt, h) for h in heads]

    has_prev = qi > 0
    prev = jnp.maximum(qi - 1, 0)
    z_own = [_dot(_key_block(k_ref, qi), qhs[h]) for h in heads]
    z_prev = [_dot(_key_block(k_ref, prev), qhs[h]) for h in heads]
    sp_own = [jnp.where(strict, _softplus2(z_own[h]), 0.0) for h in heads]
    sp_prev = [_softplus2(z_prev[h]) for h in heads]
    tail_own = [_suffix_sums(later_keys, sp_own[h]) for h in heads]
    tail_prev = [_suffix_sums(later_keys, sp_prev[h]) for h in heads]
    accs, carries = [], []
    for h in heads:
        w_own = jnp.where(strict, jnp.exp2(z_own[h] - sp_own[h] - tail_own[h]), 0.0)
        carry = jnp.where(has_prev, tail_own[h][0:1, :] + sp_own[h][0:1, :], jnp.inf)
        w_prev = jnp.exp2(z_prev[h] - sp_prev[h] - tail_prev[h] - carry)
        accs.append(_dot(_value_block(vt_ref, qi, h), w_own.astype(BF16))
                    + _dot(_value_block(vt_ref, prev, h), w_prev.astype(BF16)))
        carries.append(carry + tail_prev[h][0:1, :] + sp_prev[h][0:1, :])

    def carry_min(carries):
        return jnp.min(jnp.minimum(carries[0], carries[1]))

    def cond(state):
        j, cmin, _, _ = state
        return (j >= 0) & (cmin < SB_LOG2_WEIGHT_FLOOR)

    def body(state):
        j, _, carries, accs = state
        new_carries, new_accs = [], []
        for h in heads:
            z2 = _dot(_key_block(k_ref, j), qhs[h])
            sp = _softplus2(z2)
            tail = _suffix_sums(later_keys, sp) + carries[h]
            w = jnp.exp2(z2 - sp - tail)
            new_accs.append(accs[h] + _dot(_value_block(vt_ref, j, h), w.astype(BF16)))
            new_carries.append(tail[0:1, :] + sp[0:1, :])
        return j - 1, carry_min(new_carries), tuple(new_carries), tuple(new_accs)

    _, _, _, accs = lax.while_loop(cond, body, (qi - 2, carry_min(carries), tuple(carries), tuple(accs)))

    y_ref[0] = jnp.concatenate(accs, axis=0).T.astype(BF16)


def _sb_attention(qt, k, vt):
    b, t, _ = k.shape
    grid, in_specs, out_spec = _attention_specs(b, t)
    return pl.pallas_call(
        _sb_kernel,
        grid=grid,
        in_specs=in_specs,
        out_specs=out_spec,
        out_shape=jax.ShapeDtypeStruct((b, t, W_ATT), BF16),
        compiler_params=pltpu.CompilerParams(
            dimension_semantics=("parallel", "parallel", "arbitrary"), vmem_limit_bytes=VMEM_LIMIT),
        name="stickbreak_attention",
    )(qt, k, vt)


def _mix_kernel(x_ref, ya_ref, yb_ref, gate_ref, wa_ref, wb_ref, wo_ref, o_ref):
    d = x_ref.shape[1]
    ga = gate_ref[:, :d].astype(F32)
    gb = gate_ref[:, d:].astype(F32)
    mixed = ga * _dot(ya_ref[...], wa_ref[...]) + gb * _dot(yb_ref[...], wb_ref[...])
    o_ref[...] = x_ref[...] + _dot(mixed.astype(BF16), wo_ref[...])


def _mix(x2, ya, yb, gate, wa, wb, wo):
    n, d = x2.shape
    tm = ROW_TILE
    row = lambda width: pl.BlockSpec((tm, width), lambda i: (i, 0))
    full = lambda a: pl.BlockSpec(a.shape, lambda i: (0, 0))
    return pl.pallas_call(
        _mix_kernel,
        grid=(n // tm,),
        in_specs=[row(d), row(W_ATT), row(W_ATT), row(2 * d), full(wa), full(wb), full(wo)],
        out_specs=row(d),
        out_shape=jax.ShapeDtypeStruct((n, d), F32),
        compiler_params=pltpu.CompilerParams(
            dimension_semantics=("parallel",), vmem_limit_bytes=VMEM_LIMIT),
        name="mix_outproj",
    )(x2, ya, yb, gate, wa, wb, wo)


def _ffn_kernel(x_ref, g_ref, wup_ref, cw_ref, cb_ref, wdown_ref, gfin_ref, o_ref, tail_ref, act_ref,
                *, tiles_per_seq, final_norm):
    tm = x_ref.shape[0]
    d_ff = wdown_ref.shape[0]

    @pl.when(pl.program_id(0) % tiles_per_seq == 0)
    def _():
        tail_ref[...] = jnp.zeros_like(tail_ref)

    x = x_ref[...]
    h = _rms_normalize(x, g_ref[...]).astype(BF16)
    row = lax.broadcasted_iota(jnp.int32, (tm, FF_CHUNK), 0)

    def conv_chunk(lo):
        u = _dot(h, wup_ref[:, lo:lo + FF_CHUNK])
        prev = tail_ref[:, lo:lo + FF_CHUNK]
        tail_ref[:, lo:lo + FF_CHUNK] = u[tm - TAIL_ROWS:, :]
        p1 = prev[TAIL_ROWS - 1:TAIL_ROWS, :]
        p2 = prev[TAIL_ROWS - 2:TAIL_ROWS - 1, :]
        u1 = jnp.where(row == 0, p1, pltpu.roll(u, 1, 0))
        u2 = jnp.where(row == 0, p2, jnp.where(row == 1, p1, pltpu.roll(u, 2, 0)))
        cw = cw_ref[:, lo:lo + FF_CHUNK]
        return cw[0:1] * u2 + cw[1:2] * u1 + cw[2:3] * u + cb_ref[:, lo:lo + FF_CHUNK]

    for c in range(d_ff // FF_CHUNK):
        a = conv_chunk(c * FF_CHUNK)
        bgate = conv_chunk(d_ff + c * FF_CHUNK)
        act_ref[:, c * FF_CHUNK:(c + 1) * FF_CHUNK] = (a * jax.nn.sigmoid(a) * bgate).astype(BF16)

    y = x + _dot(act_ref[...], wdown_ref[...])
    if final_norm:
        y = _rms_normalize(y, gfin_ref[...])
    o_ref[...] = y


def _ffn(x2, g, wup, cw, cb, wdown, gfin, *, seq_len, final_norm):
    n, d = x2.shape
    d_ff = wdown.shape[0]
    tm = ROW_TILE
    row = pl.BlockSpec((tm, d), lambda i: (i, 0))
    full = lambda a: pl.BlockSpec(a.shape, lambda i: (0, 0))
    return pl.pallas_call(
        functools.partial(_ffn_kernel, tiles_per_seq=seq_len // tm, final_norm=final_norm),
        grid=(n // tm,),
        in_specs=[row, full(g), full(wup), full(cw), full(cb), full(wdown), full(gfin)],
        out_specs=row,
        out_shape=jax.ShapeDtypeStruct((n, d), F32),
        scratch_shapes=[pltpu.VMEM((TAIL_ROWS, 2 * d_ff), F32), pltpu.VMEM((tm, d_ff), BF16)],
        compiler_params=pltpu.CompilerParams(
            dimension_semantics=("arbitrary",), vmem_limit_bytes=VMEM_LIMIT),
        name="conv_ffn",
    )(x2, g, wup, cw, cb, wdown, gfin)


def kernel(x, g_mix, w_in, w_proj_moba, w_proj_sb, w_out, g_ffn, w_up, conv_w, conv_b, w_down, g_final):
    b, t, d = x.shape
    depth = w_in.shape[0]
    assert t % ROW_TILE == 0 and ROW_TILE % KEY_BLOCK == 0 and Q_TILE == KEY_BLOCK
    assert w_up.shape[2] % (2 * FF_CHUNK) == 0 and conv_w.shape[1] == CONV_WIDTH
    slopes = jnp.exp2(-8.0 * jnp.arange(1, N_HEADS + 1, dtype=F32) / N_HEADS)
    gfin = g_final.reshape(1, d)
    for layer in range(depth):
        qta, ka, vta, qtb, kb, vtb, gate = _inproj(x, g_mix[layer], w_in[layer].astype(BF16))
        ya = _moba_attention(slopes, qta, ka, vta)
        yb = _sb_attention(qtb, kb, vtb)
        x1 = _mix(x.reshape(b * t, d), ya.reshape(b * t, W_ATT), yb.reshape(b * t, W_ATT),
                  gate.reshape(b * t, 2 * d), w_proj_moba[layer].astype(BF16),
                  w_proj_sb[layer].astype(BF16), w_out[layer].astype(BF16))
        x = _ffn(x1, g_ffn[layer].reshape(1, d), w_up[layer].astype(BF16),
                 conv_w[layer].reshape(CONV_WIDTH, -1), conv_b[layer].reshape(1, -1),
                 w_down[layer].astype(BF16), gfin, seq_len=t,
                 final_norm=(layer == depth - 1)).reshape(b, t, d)
    return x
```

```python
import functools

import jax
import jax.numpy as jnp
from jax import lax
from jax.experimental import pallas as pl
from jax.experimental.pallas import tpu as pltpu

F32 = jnp.float32
BF16 = jnp.bfloat16

HEAD_DIM = 64
N_HEADS = 8
W_ATT = N_HEADS * HEAD_DIM
HEADS_PER_STEP = 2
LANES = 128
KEY_BLOCK = 256
Q_TILE = 256
MOBA_TOPK = 3
MOBA_UNROLL = 2
LOG2E = 1.4426950408889634
RMS_EPS = 1e-6
CONV_WIDTH = 3
FF_CHUNK = 256
ROW_TILE = 512
TAIL_ROWS = 8
SB_LOG2_WEIGHT_FLOOR = 160.0
VMEM_LIMIT = 56 * 1024 * 1024


def _dot(a, b):
    return jnp.dot(a, b, preferred_element_type=F32)


def _rms_normalize(x, g):
    return x * lax.rsqrt(jnp.mean(x * x, axis=-1, keepdims=True) + RMS_EPS) * g


def _inproj_kernel(x_ref, g_ref, w_ref, qa_ref, ka_ref, va_ref, qb_ref, kb_ref, vb_ref, gate_ref):
    h = _rms_normalize(x_ref[0], g_ref[...]).astype(BF16)
    scale = HEAD_DIM ** -0.5

    def proj(lo, hi):
        return _dot(h, w_ref[:, lo:hi])

    def put_blocks(ref, t):
        for j in range(t.shape[1] // KEY_BLOCK):
            ref[0, j] = t[:, j * KEY_BLOCK:(j + 1) * KEY_BLOCK].astype(BF16)

    w = W_ATT
    qa_ref[0] = (proj(0, w) * (scale * LOG2E)).T.astype(BF16)
    ka_ref[0] = proj(w, 2 * w).astype(BF16)
    put_blocks(va_ref, proj(2 * w, 3 * w).T)
    qb_ref[0] = (proj(3 * w, 4 * w) * (scale * LOG2E)).T.astype(BF16)
    kb_ref[0] = proj(4 * w, 5 * w).astype(BF16)
    put_blocks(vb_ref, proj(5 * w, 6 * w).T)
    gate_ref[0] = jax.nn.sigmoid(proj(6 * w, w_ref.shape[1])).astype(BF16)


def _inproj(x, g, w_in):
    b, t, d = x.shape
    d_in = w_in.shape[1]
    tm = ROW_TILE
    nblk = t // KEY_BLOCK
    qt_shape = jax.ShapeDtypeStruct((b, W_ATT, t), BF16)
    k_shape = jax.ShapeDtypeStruct((b, t, W_ATT), BF16)
    vt_shape = jax.ShapeDtypeStruct((b, nblk, W_ATT, KEY_BLOCK), BF16)
    gate_shape = jax.ShapeDtypeStruct((b, t, d_in - 6 * W_ATT), BF16)
    qt_spec = pl.BlockSpec((1, W_ATT, tm), lambda bi, i: (bi, 0, i))
    k_spec = pl.BlockSpec((1, tm, W_ATT), lambda bi, i: (bi, i, 0))
    vt_spec = pl.BlockSpec((1, tm // KEY_BLOCK, W_ATT, KEY_BLOCK), lambda bi, i: (bi, i, 0, 0))
    return pl.pallas_call(
        _inproj_kernel,
        grid=(b, t // tm),
        in_specs=[
            pl.BlockSpec((1, tm, d), lambda bi, i: (bi, i, 0)),
            pl.BlockSpec((1, d), lambda bi, i: (0, 0)),
            pl.BlockSpec((d, d_in), lambda bi, i: (0, 0)),
        ],
        out_specs=[qt_spec, k_spec, vt_spec, qt_spec, k_spec, vt_spec,
                   pl.BlockSpec((1, tm, d_in - 6 * W_ATT), lambda bi, i: (bi, i, 0))],
        out_shape=[qt_shape, k_shape, vt_shape, qt_shape, k_shape, vt_shape, gate_shape],
        compiler_params=pltpu.CompilerParams(
            dimension_semantics=("parallel", "parallel"), vmem_limit_bytes=VMEM_LIMIT),
        name="inproj",
    )(x, g.reshape(1, d), w_in)


def _head_queries(qt, h):
    row = lax.broadcasted_iota(jnp.int32, qt.shape, 0)
    keep = (row >= h * HEAD_DIM) & (row < (h + 1) * HEAD_DIM)
    return jnp.where(keep, qt.astype(F32), 0.0).astype(BF16)


def _key_block(k_ref, j):
    return k_ref[0, pl.ds(pl.multiple_of(j * KEY_BLOCK, KEY_BLOCK), KEY_BLOCK), :]


def _value_block(vt_ref, j, h):
    return vt_ref[0, j, h * HEAD_DIM:(h + 1) * HEAD_DIM, :]


def _moba_kernel(slopes_ref, qt_ref, k_ref, vt_ref, y_ref, kmean_ref, rowbias_ref, tile_ref, s_ref, *, nblk):
    hp = pl.program_id(1)
    qi = pl.program_id(2)
    tq = Q_TILE

    @pl.when(qi == 0)
    def _():
        def body(j, c):
            blk = _key_block(k_ref, j).astype(F32)
            kmean_ref[pl.ds(j, 1), :] = jnp.sum(blk, axis=0, keepdims=True) * (1.0 / KEY_BLOCK)
            return c
        lax.fori_loop(0, nblk, body, 0)

    km = kmean_ref[...]
    km_hi = km.astype(BF16)
    km_r = km - km_hi.astype(F32)
    km_mid = km_r.astype(BF16)
    km_lo = (km_r - km_mid.astype(F32)).astype(BF16)

    qt = qt_ref[0]
    blk_id = lax.broadcasted_iota(jnp.int32, (nblk, tq), 0).astype(F32)
    key_in_blk = lax.broadcasted_iota(jnp.int32, (KEY_BLOCK, tq), 0)
    qry_in_blk = lax.broadcasted_iota(jnp.int32, (KEY_BLOCK, tq), 1)
    rel = (key_in_blk - qry_in_blk).astype(F32)
    own = qi.astype(F32)
    neg_inf = -jnp.inf

    heads = range(HEADS_PER_STEP)
    qhs = []
    for h in heads:
        slope2 = slopes_ref[hp * HEADS_PER_STEP + h] * LOG2E
        qh = _head_queries(qt, h)
        gate = _dot(km_hi, qh) + _dot(km_mid, qh) + _dot(km_lo, qh)
        gate = jnp.where(blk_id < own, gate, neg_inf)
        chosen = jnp.where(blk_id == own, 1.0, 0.0)
        for _ in range(min(MOBA_TOPK, nblk)):
            best = jnp.max(gate, axis=0, keepdims=True)
            first = jnp.min(jnp.where(gate == best, blk_id, float(nblk)), axis=0, keepdims=True)
            hit = blk_id == first
            chosen = jnp.where(hit & (best > neg_inf), 1.0, chosen)
            gate = jnp.where(hit, neg_inf, gate)
        rowbias_ref[h, :nblk] = jnp.where(chosen > 0.0, slope2 * KEY_BLOCK * (blk_id - own), neg_inf)
        rowbias_ref[h, nblk:] = jnp.full((TAIL_ROWS, tq), neg_inf, F32)
        alibi = slope2 * rel
        tile_ref[h, 0] = alibi
        tile_ref[h, 1] = jnp.where(key_in_blk <= qry_in_blk, alibi, neg_inf)
        qhs.append(qh)

    def position(t):
        t = jnp.asarray(t, jnp.int32)
        is_own = t == 0
        kidx = jnp.where(is_own, qi, jnp.minimum(t - 1, jnp.maximum(qi - 1, 0)))
        ridx = jnp.where(is_own, qi, jnp.where(t <= qi, t - 1, nblk))
        return is_own.astype(jnp.int32), kidx, ridx

    def score_stage(i, slot):
        maxes = []
        for g in range(MOBA_UNROLL):
            kind, kidx, ridx = position(i * MOBA_UNROLL + g)
            kblk = _key_block(k_ref, kidx)
            for h in heads:
                s = _dot(kblk, qhs[h]) + tile_ref[h, kind]
                s_ref[slot, g * HEADS_PER_STEP + h] = s
                maxes.append(jnp.max(s, axis=0, keepdims=True) + rowbias_ref[h, pl.ds(ridx, 1), :])
        return tuple(maxes)

    def value_stage(i, slot, maxes, states):
        new_states = []
        for h in heads:
            m, l, acc = states[h]
            m_new = m
            for g in range(MOBA_UNROLL):
                m_new = jnp.maximum(m_new, maxes[g * HEADS_PER_STEP + h])
            alpha = jnp.exp2(m - m_new)
            l = alpha * l
            acc = alpha * acc
            for g in range(MOBA_UNROLL):
                _, kidx, ridx = position(i * MOBA_UNROLL + g)
                shift = m_new - rowbias_ref[h, pl.ds(ridx, 1), :]
                p = jnp.exp2(s_ref[slot, g * HEADS_PER_STEP + h] - shift)
                l = l + jnp.sum(p, axis=0, keepdims=True)
                acc = acc + _dot(_value_block(vt_ref, kidx, h), p.astype(BF16))
            new_states.append((m_new, l, acc))
        return tuple(new_states)

    def body(ii, carry):
        maxes0, states = carry
        i = 2 * ii
        maxes1 = score_stage(i + 1, 1)
        states = value_stage(i, 0, maxes0, states)
        maxes2 = score_stage(i + 2, 0)
        states = value_stage(i + 1, 1, maxes1, states)
        return maxes2, states

    n_steps = (qi + MOBA_UNROLL) // MOBA_UNROLL
    n_pairs = (n_steps - 1) // 2
    init = tuple((jnp.full((1, tq), neg_inf, F32), jnp.zeros((1, tq), F32), jnp.zeros((HEAD_DIM, tq), F32))
                 for _ in heads)
    maxes0, states = lax.fori_loop(0, n_pairs, body, (score_stage(0, 0), init))
    last = 2 * n_pairs

    def two_steps(maxes0, states):
        maxes1 = score_stage(last + 1, 1)
        states = value_stage(last, 0, maxes0, states)
        return value_stage(last + 1, 1, maxes1, states)

    def one_step(maxes0, states):
        return value_stage(last, 0, maxes0, states)

    states = lax.cond(last + 1 < n_steps, two_steps, one_step, maxes0, states)
    outs = [acc / l for _, l, acc in states]

    y_ref[0] = jnp.concatenate(outs, axis=0).T.astype(BF16)


def _attention_specs(b, t):
    nblk = t // KEY_BLOCK
    width = HEADS_PER_STEP * HEAD_DIM
    in_specs = [
        pl.BlockSpec((1, width, Q_TILE), lambda bi, hp, qi: (bi, hp, qi)),
        pl.BlockSpec((1, t, width), lambda bi, hp, qi: (bi, 0, hp)),
        pl.BlockSpec((1, nblk, width, KEY_BLOCK), lambda bi, hp, qi: (bi, 0, hp, 0)),
    ]
    out_spec = pl.BlockSpec((1, Q_TILE, width), lambda bi, hp, qi: (bi, qi, hp))
    grid = (b, N_HEADS // HEADS_PER_STEP, t // Q_TILE)
    return grid, in_specs, out_spec


def _moba_attention(slopes, qt, k, vt):
    b, t, _ = k.shape
    nblk = t // KEY_BLOCK
    grid, in_specs, out_spec = _attention_specs(b, t)
    return pl.pallas_call(
        functools.partial(_moba_kernel, nblk=nblk),
        grid=grid,
        in_specs=[pl.BlockSpec(memory_space=pltpu.SMEM)] + in_specs,
        out_specs=out_spec,
        out_shape=jax.ShapeDtypeStruct((b, t, W_ATT), BF16),
        scratch_shapes=[pltpu.VMEM((nblk, HEADS_PER_STEP * HEAD_DIM), F32),
                        pltpu.VMEM((HEADS_PER_STEP, nblk + TAIL_ROWS, Q_TILE), F32),
                        pltpu.VMEM((HEADS_PER_STEP, 2, KEY_BLOCK, Q_TILE), F32),
                        pltpu.VMEM((2, MOBA_UNROLL * HEADS_PER_STEP, KEY_BLOCK, Q_TILE), F32)],
        compiler_params=pltpu.CompilerParams(
            dimension_semantics=("parallel", "parallel", "arbitrary"), vmem_limit_bytes=VMEM_LIMIT),
        name="moba_attention",
    )(slopes, qt, k, vt)


def _softplus2(z2):
    return jnp.maximum(z2, 0.0) + jnp.log2(1.0 + jnp.exp2(-jnp.abs(z2)))


def _suffix_sums(later_keys, x):
    return _dot(later_keys, x.astype(BF16))


def _sb_kernel(qt_ref, k_ref, vt_ref, y_ref):
    qi = pl.program_id(2)
    tq = Q_TILE
    qt = qt_ref[0]
    key_in_blk = lax.broadcasted_iota(jnp.int32, (KEY_BLOCK, tq), 0)
    qry_in_blk = lax.broadcasted_iota(jnp.int32, (KEY_BLOCK, tq), 1)
    strict = key_in_blk < qry_in_blk
    r = lax.broadcasted_iota(jnp.int32, (KEY_BLOCK, KEY_BLOCK), 0)
    c = lax.broadcasted_iota(jnp.int32, (KEY_BLOCK, KEY_BLOCK), 1)
    later_keys = jnp.where(c > r, 1.0, 0.0).astype(BF16)

    heads = range(HEADS_PER_STEP)
    qhs = [_head_queries(qt, h) for h in heads]

    has_prev = qi > 0
    prev = jnp.maximum(qi - 1, 0)
    z_own = [_dot(_key_block(k_ref, qi), qhs[h]) for h in heads]
    z_prev = [_dot(_key_block(k_ref, prev), qhs[h]) for h in heads]
    sp_own = [jnp.where(strict, _softplus2(z_own[h]), 0.0) for h in heads]
    sp_prev = [_softplus2(z_prev[h]) for h in heads]
    tail_own = [_suffix_sums(later_keys, sp_own[h]) for h in heads]
    tail_prev = [_suffix_sums(later_keys, sp_prev[h]) for h in heads]
    accs, carries = [], []
    for h in heads:
        w_own = jnp.where(strict, jnp.exp2(z_own[h] - sp_own[h] - tail_own[h]), 0.0)
        carry = jnp.where(has_prev, tail_own[h][0:1, :] + sp_own[h][0:1, :], jnp.inf)
        w_prev = jnp.exp2(z_prev[h] - sp_prev[h] - tail_prev[h] - carry)
        accs.append(_dot(_value_block(vt_ref, qi, h), w_own.astype(BF16))
                    + _dot(_value_block(vt_ref, prev, h), w_prev.astype(BF16)))
        carries.append(carry + tail_prev[h][0:1, :] + sp_prev[h][0:1, :])

    def carry_min(carries):
        return jnp.min(jnp.minimum(carries[0], carries[1]))

    def cond(state):
        j, cmin, _, _ = state
        return (j >= 0) & (cmin < SB_LOG2_WEIGHT_FLOOR)

    def body(state):
        j, _, carries, accs = state
        new_carries, new_accs = [], []
        for h in heads:
            z2 = _dot(_key_block(k_ref, j), qhs[h])
            sp = _softplus2(z2)
            tail = _suffix_sums(later_keys, sp) + carries[h]
            w = jnp.exp2(z2 - sp - tail)
            new_accs.append(accs[h] + _dot(_value_block(vt_ref, j, h), w.astype(BF16)))
            new_carries.append(tail[0:1, :] + sp[0:1, :])
        return j - 1, carry_min(new_carries), tuple(new_carries), tuple(new_accs)

    _, _, _, accs = lax.while_loop(cond, body, (qi - 2, carry_min(carries), tuple(carries), tuple(accs)))

    y_ref[0] = jnp.concatenate(accs, axis=0).T.astype(BF16)


def _sb_attention(qt, k, vt):
    b, t, _ = k.shape
    grid, in_specs, out_spec = _attention_specs(b, t)
    return pl.pallas_call(
        _sb_kernel,
        grid=grid,
        in_specs=in_specs,
        out_specs=out_spec,
        out_shape=jax.ShapeDtypeStruct((b, t, W_ATT), BF16),
        compiler_params=pltpu.CompilerParams(
            dimension_semantics=("parallel", "parallel", "arbitrary"), vmem_limit_bytes=VMEM_LIMIT),
        name="stickbreak_attention",
    )(qt, k, vt)


def _post_kernel(x_ref, ya_ref, yb_ref, gate_ref, wa_ref, wb_ref, wo_ref, g_ref, wup_ref, cw_ref, cb_ref,
                 wdown_ref, gfin_ref, o_ref, tail_ref, act_ref, *, tiles_per_seq, final_norm):
    tm, d = x_ref.shape
    d_ff = wdown_ref.shape[0]

    @pl.when(pl.program_id(0) % tiles_per_seq == 0)
    def _():
        tail_ref[...] = jnp.zeros_like(tail_ref)

    ga = gate_ref[:, :d].astype(F32)
    gb = gate_ref[:, d:].astype(F32)
    mixed = ga * _dot(ya_ref[...], wa_ref[...]) + gb * _dot(yb_ref[...], wb_ref[...])
    x = x_ref[...] + _dot(mixed.astype(BF16), wo_ref[...])
    o_ref[...] = x
    h = _rms_normalize(x, g_ref[...]).astype(BF16)
    row = lax.broadcasted_iota(jnp.int32, (tm, FF_CHUNK), 0)

    def conv_chunk(lo):
        u = _dot(h, wup_ref[:, lo:lo + FF_CHUNK])
        prev = tail_ref[:, lo:lo + FF_CHUNK]
        tail_ref[:, lo:lo + FF_CHUNK] = u[tm - TAIL_ROWS:, :]
        p1 = prev[TAIL_ROWS - 1:TAIL_ROWS, :]
        p2 = prev[TAIL_ROWS - 2:TAIL_ROWS - 1, :]
        u1 = jnp.where(row == 0, p1, pltpu.roll(u, 1, 0))
        u2 = jnp.where(row == 0, p2, jnp.where(row == 1, p1, pltpu.roll(u, 2, 0)))
        cw = cw_ref[:, lo:lo + FF_CHUNK]
        return cw[0:1] * u2 + cw[1:2] * u1 + cw[2:3] * u + cb_ref[:, lo:lo + FF_CHUNK]

    for c in range(d_ff // FF_CHUNK):
        a = conv_chunk(c * FF_CHUNK)
        bgate = conv_chunk(d_ff + c * FF_CHUNK)
        act_ref[:, c * FF_CHUNK:(c + 1) * FF_CHUNK] = (a * jax.nn.sigmoid(a) * bgate).astype(BF16)

    y = o_ref[...] + _dot(act_ref[...], wdown_ref[...])
    if final_norm:
        y = _rms_normalize(y, gfin_ref[...])
    o_ref[...] = y


def _post(x2, ya, yb, gate, wa, wb, wo, g, wup, cw, cb, wdown, gfin, *, seq_len, final_norm):
    n, d = x2.shape
    d_ff = wdown.shape[0]
    tm = ROW_TILE
    row = lambda width: pl.BlockSpec((tm, width), lambda i: (i, 0))
    full = lambda a: pl.BlockSpec(a.shape, lambda i: (0, 0), pipeline_mode=pl.Buffered(1))
    return pl.pallas_call(
        functools.partial(_post_kernel, tiles_per_seq=seq_len // tm, final_norm=final_norm),
        grid=(n // tm,),
        in_specs=[row(d), row(W_ATT), row(W_ATT), row(2 * d), full(wa), full(wb), full(wo),
                  full(g), full(wup), full(cw), full(cb), full(wdown), full(gfin)],
        out_specs=row(d),
        out_shape=jax.ShapeDtypeStruct((n, d), F32),
        scratch_shapes=[pltpu.VMEM((TAIL_ROWS, 2 * d_ff), F32), pltpu.VMEM((tm, d_ff), BF16)],
        compiler_params=pltpu.CompilerParams(
            dimension_semantics=("arbitrary",), vmem_limit_bytes=VMEM_LIMIT),
        name="mix_conv_ffn",
    )(x2, ya, yb, gate, wa, wb, wo, g, wup, cw, cb, wdown, gfin)


def kernel(x, g_mix, w_in, w_proj_moba, w_proj_sb, w_out, g_ffn, w_up, conv_w, conv_b, w_down, g_final):
    b, t, d = x.shape
    depth = w_in.shape[0]
    assert t % ROW_TILE == 0 and ROW_TILE % KEY_BLOCK == 0 and Q_TILE == KEY_BLOCK
    assert w_up.shape[2] % (2 * FF_CHUNK) == 0 and conv_w.shape[1] == CONV_WIDTH
    slopes = jnp.exp2(-8.0 * jnp.arange(1, N_HEADS + 1, dtype=F32) / N_HEADS)
    gfin = g_final.reshape(1, d)
    for layer in range(depth):
        qta, ka, vta, qtb, kb, vtb, gate = _inproj(x, g_mix[layer], w_in[layer].astype(BF16))
        ya = _moba_attention(slopes, qta, ka, vta)
        yb = _sb_attention(qtb, kb, vtb)
        x = _post(x.reshape(b * t, d), ya.reshape(b * t, W_ATT), yb.reshape(b * t, W_ATT),
                  gate.reshape(b * t, 2 * d), w_proj_moba[layer].astype(BF16),
                  w_proj_sb[layer].astype(BF16), w_out[layer].astype(BF16),
                  g_ffn[layer].reshape(1, d), w_up[layer].astype(BF16),
                  conv_w[layer].reshape(CONV_WIDTH, -1), conv_b[layer].reshape(1, -1),
                  w_down[layer].astype(BF16), gfin, seq_len=t,
                  final_norm=(layer == depth - 1)).reshape(b, t, d)
    return x
```

```python
import functools

import jax
import jax.numpy as jnp
from jax import lax
from jax.experimental import pallas as pl
from jax.experimental.pallas import tpu as pltpu

F32 = jnp.float32
BF16 = jnp.bfloat16

HEAD_DIM = 64
N_HEADS = 8
W_ATT = N_HEADS * HEAD_DIM
LANES = 128
HEADS_PER_GROUP = LANES // HEAD_DIM
HEADS_PER_STEP = 4
KEY_BLOCK = 256
Q_TILE = 256
MOBA_TOPK = 3
MOBA_UNROLL = 1
LOG2E = 1.4426950408889634
RMS_EPS = 1e-6
CONV_WIDTH = 3
FF_CHUNK = 256
ROW_TILE = 512
TAIL_ROWS = 8
SB_LOG2_WEIGHT_FLOOR = 160.0
VMEM_LIMIT = 56 * 1024 * 1024


def _dot(a, b):
    return jnp.dot(a, b, preferred_element_type=F32)


def _rms_normalize(x, g):
    return x * lax.rsqrt(jnp.mean(x * x, axis=-1, keepdims=True) + RMS_EPS) * g


def _inproj_kernel(x_ref, g_ref, w_ref, qa_ref, ka_ref, va_ref, qb_ref, kb_ref, vb_ref, gate_ref):
    h = _rms_normalize(x_ref[0], g_ref[...]).astype(BF16)
    scale = HEAD_DIM ** -0.5

    def proj(lo, hi):
        return _dot(h, w_ref[:, lo:hi])

    def put_blocks(ref, t):
        for j in range(t.shape[1] // KEY_BLOCK):
            ref[0, j] = t[:, j * KEY_BLOCK:(j + 1) * KEY_BLOCK].astype(BF16)

    w = W_ATT
    qa_ref[0] = (proj(0, w) * (scale * LOG2E)).T.astype(BF16)
    ka_ref[0] = proj(w, 2 * w).astype(BF16)
    put_blocks(va_ref, proj(2 * w, 3 * w).T)
    qb_ref[0] = (proj(3 * w, 4 * w) * (scale * LOG2E)).T.astype(BF16)
    kb_ref[0] = proj(4 * w, 5 * w).astype(BF16)
    put_blocks(vb_ref, proj(5 * w, 6 * w).T)
    gate_ref[0] = jax.nn.sigmoid(proj(6 * w, w_ref.shape[1])).astype(BF16)


def _inproj(x, g, w_in):
    b, t, d = x.shape
    d_in = w_in.shape[1]
    tm = ROW_TILE
    nblk = t // KEY_BLOCK
    qt_shape = jax.ShapeDtypeStruct((b, W_ATT, t), BF16)
    k_shape = jax.ShapeDtypeStruct((b, t, W_ATT), BF16)
    vt_shape = jax.ShapeDtypeStruct((b, nblk, W_ATT, KEY_BLOCK), BF16)
    gate_shape = jax.ShapeDtypeStruct((b, t, d_in - 6 * W_ATT), BF16)
    qt_spec = pl.BlockSpec((1, W_ATT, tm), lambda bi, i: (bi, 0, i))
    k_spec = pl.BlockSpec((1, tm, W_ATT), lambda bi, i: (bi, i, 0))
    vt_spec = pl.BlockSpec((1, tm // KEY_BLOCK, W_ATT, KEY_BLOCK), lambda bi, i: (bi, i, 0, 0))
    return pl.pallas_call(
        _inproj_kernel,
        grid=(b, t // tm),
        in_specs=[
            pl.BlockSpec((1, tm, d), lambda bi, i: (bi, i, 0)),
            pl.BlockSpec((1, d), lambda bi, i: (0, 0)),
            pl.BlockSpec((d, d_in), lambda bi, i: (0, 0)),
        ],
        out_specs=[qt_spec, k_spec, vt_spec, qt_spec, k_spec, vt_spec,
                   pl.BlockSpec((1, tm, d_in - 6 * W_ATT), lambda bi, i: (bi, i, 0))],
        out_shape=[qt_shape, k_shape, vt_shape, qt_shape, k_shape, vt_shape, gate_shape],
        compiler_params=pltpu.CompilerParams(
            dimension_semantics=("parallel", "parallel"), vmem_limit_bytes=VMEM_LIMIT),
        name="inproj",
    )(x, g.reshape(1, d), w_in)


def _lane_group(h):
    g = h // HEADS_PER_GROUP
    return slice(g * LANES, (g + 1) * LANES)


def _head_queries(qt, h):
    grp = qt[_lane_group(h), :].astype(F32)
    row = lax.broadcasted_iota(jnp.int32, grp.shape, 0)
    lo = (h % HEADS_PER_GROUP) * HEAD_DIM
    keep = (row >= lo) & (row < lo + HEAD_DIM)
    return jnp.where(keep, grp, 0.0).astype(BF16)


def _key_block(k_ref, j, h):
    rows = pl.ds(pl.multiple_of(j * KEY_BLOCK, KEY_BLOCK), KEY_BLOCK)
    return k_ref[0, rows, _lane_group(h)]


def _value_block(vt_ref, j, h):
    return vt_ref[0, j, h * HEAD_DIM:(h + 1) * HEAD_DIM, :]


def _moba_kernel(slopes_ref, qt_ref, k_ref, vt_ref, y_ref, kmean_ref, rowbias_ref, tile_ref, s_ref, *, nblk):
    hp = pl.program_id(1)
    qi = pl.program_id(2)
    tq = Q_TILE

    heads = range(HEADS_PER_STEP)
    neg_inf = -jnp.inf

    def slope2(h):
        return slopes_ref[hp * HEADS_PER_STEP + h] * LOG2E

    @pl.when(qi == 0)
    def _():
        def body(j, c):
            rows = pl.ds(pl.multiple_of(j * KEY_BLOCK, KEY_BLOCK), KEY_BLOCK)
            blk = k_ref[0, rows, :].astype(F32)
            kmean_ref[pl.ds(j, 1), :] = jnp.sum(blk, axis=0, keepdims=True) * (1.0 / KEY_BLOCK)
            return c
        lax.fori_loop(0, nblk, body, 0)
        key_in_blk = lax.broadcasted_iota(jnp.int32, (KEY_BLOCK, tq), 0)
        qry_in_blk = lax.broadcasted_iota(jnp.int32, (KEY_BLOCK, tq), 1)
        rel = (key_in_blk - qry_in_blk).astype(F32)
        for h in heads:
            alibi = slope2(h) * rel
            tile_ref[h, 0] = alibi
            tile_ref[h, 1] = jnp.where(key_in_blk <= qry_in_blk, alibi, neg_inf)

    km = kmean_ref[...]
    km_hi = km.astype(BF16)
    km_r = km - km_hi.astype(F32)
    km_mid = km_r.astype(BF16)
    km_lo = (km_r - km_mid.astype(F32)).astype(BF16)

    qt = qt_ref[0]
    blk_id = lax.broadcasted_iota(jnp.int32, (nblk, tq), 0).astype(F32)
    own = qi.astype(F32)

    qhs = []
    for h in heads:
        qh = _head_queries(qt, h)
        grp = _lane_group(h)
        gate = _dot(km_hi[:, grp], qh) + _dot(km_mid[:, grp], qh) + _dot(km_lo[:, grp], qh)
        gate = jnp.where(blk_id < own, gate, neg_inf)
        chosen = jnp.where(blk_id == own, 1.0, 0.0)
        for _ in range(min(MOBA_TOPK, nblk)):
            best = jnp.max(gate, axis=0, keepdims=True)
            first = jnp.min(jnp.where(gate == best, blk_id, float(nblk)), axis=0, keepdims=True)
            hit = blk_id == first
            chosen = jnp.where(hit & (best > neg_inf), 1.0, chosen)
            gate = jnp.where(hit, neg_inf, gate)
        rowbias_ref[h, :nblk] = jnp.where(chosen > 0.0, slope2(h) * KEY_BLOCK * (blk_id - own), neg_inf)
        rowbias_ref[h, nblk:] = jnp.full((TAIL_ROWS, tq), neg_inf, F32)
        qhs.append(qh)

    def position(t):
        t = jnp.asarray(t, jnp.int32)
        is_own = t == 0
        kidx = jnp.where(is_own, qi, jnp.minimum(t - 1, jnp.maximum(qi - 1, 0)))
        ridx = jnp.where(is_own, qi, jnp.where(t <= qi, t - 1, nblk))
        return is_own.astype(jnp.int32), kidx, ridx

    def score_stage(i, slot):
        maxes = []
        for g in range(MOBA_UNROLL):
            kind, kidx, ridx = position(i * MOBA_UNROLL + g)
            for h in heads:
                s = _dot(_key_block(k_ref, kidx, h), qhs[h]) + tile_ref[h, kind]
                s_ref[slot, g * HEADS_PER_STEP + h] = s
                maxes.append(jnp.max(s, axis=0, keepdims=True) + rowbias_ref[h, pl.ds(ridx, 1), :])
        return tuple(maxes)

    def value_stage(i, slot, maxes, states):
        new_states = []
        for h in heads:
            m, l, acc = states[h]
            m_new = m
            for g in range(MOBA_UNROLL):
                m_new = jnp.maximum(m_new, maxes[g * HEADS_PER_STEP + h])
            alpha = jnp.exp2(m - m_new)
            l = alpha * l
            acc = alpha * acc
            for g in range(MOBA_UNROLL):
                _, kidx, ridx = position(i * MOBA_UNROLL + g)
                shift = m_new - rowbias_ref[h, pl.ds(ridx, 1), :]
                p = jnp.exp2(s_ref[slot, g * HEADS_PER_STEP + h] - shift)
                l = l + jnp.sum(p, axis=0, keepdims=True)
                acc = acc + _dot(_value_block(vt_ref, kidx, h), p.astype(BF16))
            new_states.append((m_new, l, acc))
        return tuple(new_states)

    def body(ii, carry):
        maxes0, states = carry
        i = 2 * ii
        maxes1 = score_stage(i + 1, 1)
        states = value_stage(i, 0, maxes0, states)
        maxes2 = score_stage(i + 2, 0)
        states = value_stage(i + 1, 1, maxes1, states)
        return maxes2, states

    n_steps = (qi + MOBA_UNROLL) // MOBA_UNROLL
    n_pairs = (n_steps - 1) // 2
    init = tuple((jnp.full((1, tq), neg_inf, F32), jnp.zeros((1, tq), F32), jnp.zeros((HEAD_DIM, tq), F32))
                 for _ in heads)
    maxes0, states = lax.fori_loop(0, n_pairs, body, (score_stage(0, 0), init))
    last = 2 * n_pairs

    def two_steps(maxes0, states):
        maxes1 = score_stage(last + 1, 1)
        states = value_stage(last, 0, maxes0, states)
        return value_stage(last + 1, 1, maxes1, states)

    def one_step(maxes0, states):
        return value_stage(last, 0, maxes0, states)

    states = lax.cond(last + 1 < n_steps, two_steps, one_step, maxes0, states)
    outs = [acc / l for _, l, acc in states]

    y_ref[0] = jnp.concatenate(outs, axis=0).T.astype(BF16)


def _attention_specs(b, t):
    nblk = t // KEY_BLOCK
    width = HEADS_PER_STEP * HEAD_DIM
    in_specs = [
        pl.BlockSpec((1, width, Q_TILE), lambda bi, hp, qi: (bi, hp, qi)),
        pl.BlockSpec((1, t, width), lambda bi, hp, qi: (bi, 0, hp)),
        pl.BlockSpec((1, nblk, width, KEY_BLOCK), lambda bi, hp, qi: (bi, 0, hp, 0)),
    ]
    out_spec = pl.BlockSpec((1, Q_TILE, width), lambda bi, hp, qi: (bi, qi, hp))
    grid = (b, N_HEADS // HEADS_PER_STEP, t // Q_TILE)
    return grid, in_specs, out_spec


def _moba_attention(slopes, qt, k, vt):
    b, t, _ = k.shape
    nblk = t // KEY_BLOCK
    grid, in_specs, out_spec = _attention_specs(b, t)
    return pl.pallas_call(
        functools.partial(_moba_kernel, nblk=nblk),
        grid=grid,
        in_specs=[pl.BlockSpec(memory_space=pltpu.SMEM)] + in_specs,
        out_specs=out_spec,
        out_shape=jax.ShapeDtypeStruct((b, t, W_ATT), BF16),
        scratch_shapes=[pltpu.VMEM((nblk, HEADS_PER_STEP * HEAD_DIM), F32),
                        pltpu.VMEM((HEADS_PER_STEP, nblk + TAIL_ROWS, Q_TILE), F32),
                        pltpu.VMEM((HEADS_PER_STEP, 2, KEY_BLOCK, Q_TILE), F32),
                        pltpu.VMEM((2, MOBA_UNROLL * HEADS_PER_STEP, KEY_BLOCK, Q_TILE), F32)],
        compiler_params=pltpu.CompilerParams(
            dimension_semantics=("parallel", "parallel", "arbitrary"), vmem_limit_bytes=VMEM_LIMIT),
        name="moba_attention",
    )(slopes, qt, k, vt)


def _softplus2(z2):
    return jnp.maximum(z2, 0.0) + jnp.log2(1.0 + jnp.exp2(-jnp.abs(z2)))


def _suffix_sums(later_keys, x):
    return _dot(later_keys, x.astype(BF16))


def _sb_kernel(qt_ref, k_ref, vt_ref, y_ref):
    qi = pl.program_id(2)
    tq = Q_TILE
    qt = qt_ref[0]
    key_in_blk = lax.broadcasted_iota(jnp.int32, (KEY_BLOCK, tq), 0)
    qry_in_blk = lax.broadcasted_iota(jnp.int32, (KEY_BLOCK, tq), 1)
    strict = key_in_blk < qry_in_blk
    r = lax.broadcasted_iota(jnp.int32, (KEY_BLOCK, KEY_BLOCK), 0)
    c = lax.broadcasted_iota(jnp.int32, (KEY_BLOCK, KEY_BLOCK), 1)
    later_keys = jnp.where(c > r, 1.0, 0.0).astype(BF16)

    heads = range(HEADS_PER_STEP)
    qhs = [_head_queries(qt, h) for h in heads]

    has_prev = qi > 0
    prev = jnp.maximum(qi - 1, 0)
    z_own = [_dot(_key_block(k_ref, qi, h), qhs[h]) for h in heads]
    z_prev = [_dot(_key_block(k_ref, prev, h), qhs[h]) for h in heads]
    sp_own = [jnp.where(strict, _softplus2(z_own[h]), 0.0) for h in heads]
    sp_prev = [_softplus2(z_prev[h]) for h in heads]
    tail_own = [_suffix_sums(later_keys, sp_own[h]) for h in heads]
    tail_prev = [_suffix_sums(later_keys, sp_prev[h]) for h in heads]
    accs, carries = [], []
    for h in heads:
        w_own = jnp.where(strict, jnp.exp2(z_own[h] - sp_own[h] - tail_own[h]), 0.0)
        carry = jnp.where(has_prev, tail_own[h][0:1, :] + sp_own[h][0:1, :], jnp.inf)
        w_prev = jnp.exp2(z_prev[h] - sp_prev[h] - tail_prev[h] - carry)
        accs.append(_dot(_value_block(vt_ref, qi, h), w_own.astype(BF16))
                    + _dot(_value_block(vt_ref, prev, h), w_prev.astype(BF16)))
        carries.append(carry + tail_prev[h][0:1, :] + sp_prev[h][0:1, :])

    def carry_min(carries):
        return jnp.min(functools.reduce(jnp.minimum, carries))

    def cond(state):
        j, cmin, _, _ = state
        return (j >= 0) & (cmin < SB_LOG2_WEIGHT_FLOOR)

    def body(state):
        j, _, carries, accs = state
        new_carries, new_accs = [], []
        for h in heads:
            z2 = _dot(_key_block(k_ref, j, h), qhs[h])
            sp = _softplus2(z2)
            tail = _suffix_sums(later_keys, sp) + carries[h]
            w = jnp.exp2(z2 - sp - tail)
            new_accs.append(accs[h] + _dot(_value_block(vt_ref, j, h), w.astype(BF16)))
            new_carries.append(tail[0:1, :] + sp[0:1, :])
        return j - 1, carry_min(new_carries), tuple(new_carries), tuple(new_accs)

    _, _, _, accs = lax.while_loop(cond, body, (qi - 2, carry_min(carries), tuple(carries), tuple(accs)))

    y_ref[0] = jnp.concatenate(accs, axis=0).T.astype(BF16)


def _sb_attention(qt, k, vt):
    b, t, _ = k.shape
    grid, in_specs, out_spec = _attention_specs(b, t)
    return pl.pallas_call(
        _sb_kernel,
        grid=grid,
        in_specs=in_specs,
        out_specs=out_spec,
        out_shape=jax.ShapeDtypeStruct((b, t, W_ATT), BF16),
        compiler_params=pltpu.CompilerParams(
            dimension_semantics=("parallel", "parallel", "arbitrary"), vmem_limit_bytes=VMEM_LIMIT),
        name="stickbreak_attention",
    )(qt, k, vt)


def _post_kernel(x_ref, ya_ref, yb_ref, gate_ref, wa_ref, wb_ref, wo_ref, g_ref, wup_ref, cw_ref, cb_ref,
                 wdown_ref, gfin_ref, o_ref, tail_ref, act_ref, *, tiles_per_seq, final_norm):
    tm, d = x_ref.shape
    d_ff = wdown_ref.shape[0]

    @pl.when(pl.program_id(0) % tiles_per_seq == 0)
    def _():
        tail_ref[...] = jnp.zeros_like(tail_ref)

    ga = gate_ref[:, :d].astype(F32)
    gb = gate_ref[:, d:].astype(F32)
    mixed = ga * _dot(ya_ref[...], wa_ref[...]) + gb * _dot(yb_ref[...], wb_ref[...])
    x = x_ref[...] + _dot(mixed.astype(BF16), wo_ref[...])
    o_ref[...] = x
    h = _rms_normalize(x, g_ref[...]).astype(BF16)
    row = lax.broadcasted_iota(jnp.int32, (tm, FF_CHUNK), 0)

    def conv_chunk(lo):
        u = _dot(h, wup_ref[:, lo:lo + FF_CHUNK])
        prev = tail_ref[:, lo:lo + FF_CHUNK]
        tail_ref[:, lo:lo + FF_CHUNK] = u[tm - TAIL_ROWS:, :]
        p1 = prev[TAIL_ROWS - 1:TAIL_ROWS, :]
        p2 = prev[TAIL_ROWS - 2:TAIL_ROWS - 1, :]
        u1 = jnp.where(row == 0, p1, pltpu.roll(u, 1, 0))
        u2 = jnp.where(row == 0, p2, jnp.where(row == 1, p1, pltpu.roll(u, 2, 0)))
        cw = cw_ref[:, lo:lo + FF_CHUNK]
        return cw[0:1] * u2 + cw[1:2] * u1 + cw[2:3] * u + cb_ref[:, lo:lo + FF_CHUNK]

    for c in range(d_ff // FF_CHUNK):
        a = conv_chunk(c * FF_CHUNK)
        bgate = conv_chunk(d_ff + c * FF_CHUNK)
        act_ref[:, c * FF_CHUNK:(c + 1) * FF_CHUNK] = (a * jax.nn.sigmoid(a) * bgate).astype(BF16)

    y = o_ref[...] + _dot(act_ref[...], wdown_ref[...])
    if final_norm:
        y = _rms_normalize(y, gfin_ref[...])
    o_ref[...] = y


def _post(x2, ya, yb, gate, wa, wb, wo, g, wup, cw, cb, wdown, gfin, *, seq_len, final_norm):
    n, d = x2.shape
    d_ff = wdown.shape[0]
    tm = ROW_TILE
    row = lambda width: pl.BlockSpec((tm, width), lambda i: (i, 0))
    full = lambda a: pl.BlockSpec(a.shape, lambda i: (0, 0), pipeline_mode=pl.Buffered(1))
    return pl.pallas_call(
        functools.partial(_post_kernel, tiles_per_seq=seq_len // tm, final_norm=final_norm),
        grid=(n // tm,),
        in_specs=[row(d), row(W_ATT), row(W_ATT), row(2 * d), full(wa), full(wb), full(wo),
                  full(g), full(wup), full(cw), full(cb), full(wdown), full(gfin)],
        out_specs=row(d),
        out_shape=jax.ShapeDtypeStruct((n, d), F32),
        scratch_shapes=[pltpu.VMEM((TAIL_ROWS, 2 * d_ff), F32), pltpu.VMEM((tm, d_ff), BF16)],
        compiler_params=pltpu.CompilerParams(
            dimension_semantics=("arbitrary",), vmem_limit_bytes=VMEM_LIMIT),
        name="mix_conv_ffn",
    )(x2, ya, yb, gate, wa, wb, wo, g, wup, cw, cb, wdown, gfin)


def kernel(x, g_mix, w_in, w_proj_moba, w_proj_sb, w_out, g_ffn, w_up, conv_w, conv_b, w_down, g_final):
    b, t, d = x.shape
    depth = w_in.shape[0]
    assert t % ROW_TILE == 0 and ROW_TILE % KEY_BLOCK == 0 and Q_TILE == KEY_BLOCK
    assert w_up.shape[2] % (2 * FF_CHUNK) == 0 and conv_w.shape[1] == CONV_WIDTH
    slopes = jnp.exp2(-8.0 * jnp.arange(1, N_HEADS + 1, dtype=F32) / N_HEADS)
    gfin = g_final.reshape(1, d)
    for layer in range(depth):
        qta, ka, vta, qtb, kb, vtb, gate = _inproj(x, g_mix[layer], w_in[layer].astype(BF16))
        ya = _moba_attention(slopes, qta, ka, vta)
        yb = _sb_attention(qtb, kb, vtb)
        x = _post(x.reshape(b * t, d), ya.reshape(b * t, W_ATT), yb.reshape(b * t, W_ATT),
                  gate.reshape(b * t, 2 * d), w_proj_moba[layer].astype(BF16),
                  w_proj_sb[layer].astype(BF16), w_out[layer].astype(BF16),
                  g_ffn[layer].reshape(1, d), w_up[layer].astype(BF16),
                  conv_w[layer].reshape(CONV_WIDTH, -1), conv_b[layer].reshape(1, -1),
                  w_down[layer].astype(BF16), gfin, seq_len=t,
                  final_norm=(layer == depth - 1)).reshape(b, t, d)
    return x
```

```python
import functools

import jax
import jax.numpy as jnp
from jax import lax
from jax.experimental import pallas as pl
from jax.experimental.pallas import tpu as pltpu

F32 = jnp.float32
BF16 = jnp.bfloat16

HEAD_DIM = 64
N_HEADS = 8
W_ATT = N_HEADS * HEAD_DIM
LANES = 128
HEADS_PER_GROUP = LANES // HEAD_DIM
HEADS_PER_STEP = 4
KEY_BLOCK = 256
Q_TILE = 256
MOBA_TOPK = 3
ONES_ROWS = 16
V_ROWS = HEAD_DIM + ONES_ROWS
MOBA_UNROLL = 1
LOG2E = 1.4426950408889634
RMS_EPS = 1e-6
CONV_WIDTH = 3
FF_CHUNK = 256
ROW_TILE = 512
TAIL_ROWS = 8
SB_LOG2_WEIGHT_FLOOR = 160.0
VMEM_LIMIT = 56 * 1024 * 1024


def _dot(a, b):
    return jnp.dot(a, b, preferred_element_type=F32)


def _rms_normalize(x, g):
    return x * lax.rsqrt(jnp.mean(x * x, axis=-1, keepdims=True) + RMS_EPS) * g


def _inproj_kernel(x_ref, g_ref, w_ref, qa_ref, ka_ref, va_ref, qb_ref, kb_ref, vb_ref, gate_ref):
    h = _rms_normalize(x_ref[0], g_ref[...]).astype(BF16)
    scale = HEAD_DIM ** -0.5

    def proj(lo, hi):
        return _dot(h, w_ref[:, lo:hi])

    def put_blocks(ref, t):
        for j in range(t.shape[1] // KEY_BLOCK):
            ref[0, j] = t[:, j * KEY_BLOCK:(j + 1) * KEY_BLOCK].astype(BF16)

    def with_ones_rows(t):
        ones = jnp.ones((ONES_ROWS, t.shape[1]), F32)
        parts = []
        for hd in range(N_HEADS):
            parts += [t[hd * HEAD_DIM:(hd + 1) * HEAD_DIM], ones]
        return jnp.concatenate(parts, axis=0)

    w = W_ATT
    qa_ref[0] = (proj(0, w) * (scale * LOG2E)).T.astype(BF16)
    ka_ref[0] = proj(w, 2 * w).astype(BF16)
    put_blocks(va_ref, with_ones_rows(proj(2 * w, 3 * w).T))
    qb_ref[0] = (proj(3 * w, 4 * w) * (scale * LOG2E)).T.astype(BF16)
    kb_ref[0] = proj(4 * w, 5 * w).astype(BF16)
    put_blocks(vb_ref, proj(5 * w, 6 * w).T)
    gate_ref[0] = jax.nn.sigmoid(proj(6 * w, w_ref.shape[1])).astype(BF16)


def _inproj(x, g, w_in):
    b, t, d = x.shape
    d_in = w_in.shape[1]
    tm = ROW_TILE
    nblk = t // KEY_BLOCK
    qt_shape = jax.ShapeDtypeStruct((b, W_ATT, t), BF16)
    k_shape = jax.ShapeDtypeStruct((b, t, W_ATT), BF16)
    gate_shape = jax.ShapeDtypeStruct((b, t, d_in - 6 * W_ATT), BF16)
    qt_spec = pl.BlockSpec((1, W_ATT, tm), lambda bi, i: (bi, 0, i))
    k_spec = pl.BlockSpec((1, tm, W_ATT), lambda bi, i: (bi, i, 0))

    def vt(rows_per_head):
        rows = N_HEADS * rows_per_head
        return (jax.ShapeDtypeStruct((b, nblk, rows, KEY_BLOCK), BF16),
                pl.BlockSpec((1, tm // KEY_BLOCK, rows, KEY_BLOCK), lambda bi, i: (bi, i, 0, 0)))

    vta_shape, vta_spec = vt(HEAD_DIM + ONES_ROWS)
    vtb_shape, vtb_spec = vt(HEAD_DIM)
    return pl.pallas_call(
        _inproj_kernel,
        grid=(b, t // tm),
        in_specs=[
            pl.BlockSpec((1, tm, d), lambda bi, i: (bi, i, 0)),
            pl.BlockSpec((1, d), lambda bi, i: (0, 0)),
            pl.BlockSpec((d, d_in), lambda bi, i: (0, 0)),
        ],
        out_specs=[qt_spec, k_spec, vta_spec, qt_spec, k_spec, vtb_spec,
                   pl.BlockSpec((1, tm, d_in - 6 * W_ATT), lambda bi, i: (bi, i, 0))],
        out_shape=[qt_shape, k_shape, vta_shape, qt_shape, k_shape, vtb_shape, gate_shape],
        compiler_params=pltpu.CompilerParams(
            dimension_semantics=("parallel", "parallel"), vmem_limit_bytes=VMEM_LIMIT),
        name="inproj",
    )(x, g.reshape(1, d), w_in)


def _lane_group(h):
    g = h // HEADS_PER_GROUP
    return slice(g * LANES, (g + 1) * LANES)


def _head_queries(qt, h):
    grp = qt[_lane_group(h), :].astype(F32)
    row = lax.broadcasted_iota(jnp.int32, grp.shape, 0)
    lo = (h % HEADS_PER_GROUP) * HEAD_DIM
    keep = (row >= lo) & (row < lo + HEAD_DIM)
    return jnp.where(keep, grp, 0.0).astype(BF16)


def _key_block(k_ref, j, h):
    rows = pl.ds(pl.multiple_of(j * KEY_BLOCK, KEY_BLOCK), KEY_BLOCK)
    return k_ref[0, rows, _lane_group(h)]


def _value_block(vt_ref, j, h, rows_per_head=HEAD_DIM):
    return vt_ref[0, j, h * rows_per_head:(h + 1) * rows_per_head, :]


def _moba_kernel(slopes_ref, qt_ref, k_ref, vt_ref, y_ref, kmean_ref, rowbias_ref, tile_ref, s_ref, *, nblk):
    hp = pl.program_id(1)
    qi = pl.program_id(2)
    tq = Q_TILE

    heads = range(HEADS_PER_STEP)
    neg_inf = -jnp.inf

    def slope2(h):
        return slopes_ref[hp * HEADS_PER_STEP + h] * LOG2E

    @pl.when(qi == 0)
    def _():
        def body(j, c):
            rows = pl.ds(pl.multiple_of(j * KEY_BLOCK, KEY_BLOCK), KEY_BLOCK)
            blk = k_ref[0, rows, :].astype(F32)
            kmean_ref[pl.ds(j, 1), :] = jnp.sum(blk, axis=0, keepdims=True) * (1.0 / KEY_BLOCK)
            return c
        lax.fori_loop(0, nblk, body, 0)
        key_in_blk = lax.broadcasted_iota(jnp.int32, (KEY_BLOCK, tq), 0)
        qry_in_blk = lax.broadcasted_iota(jnp.int32, (KEY_BLOCK, tq), 1)
        rel = (key_in_blk - qry_in_blk).astype(F32)
        for h in heads:
            alibi = slope2(h) * rel
            tile_ref[h, 0] = alibi
            tile_ref[h, 1] = jnp.where(key_in_blk <= qry_in_blk, alibi, neg_inf)

    km = kmean_ref[...]
    km_hi = km.astype(BF16)
    km_r = km - km_hi.astype(F32)
    km_mid = km_r.astype(BF16)
    km_lo = (km_r - km_mid.astype(F32)).astype(BF16)

    qt = qt_ref[0]
    blk_id = lax.broadcasted_iota(jnp.int32, (nblk, tq), 0).astype(F32)
    own = qi.astype(F32)

    qhs = []
    for h in heads:
        qh = _head_queries(qt, h)
        grp = _lane_group(h)
        gate = _dot(km_hi[:, grp], qh) + _dot(km_mid[:, grp], qh) + _dot(km_lo[:, grp], qh)
        gate = jnp.where(blk_id < own, gate, neg_inf)
        chosen = jnp.where(blk_id == own, 1.0, 0.0)
        for _ in range(min(MOBA_TOPK, nblk)):
            best = jnp.max(gate, axis=0, keepdims=True)
            first = jnp.min(jnp.where(gate == best, blk_id, float(nblk)), axis=0, keepdims=True)
            hit = blk_id == first
            chosen = jnp.where(hit & (best > neg_inf), 1.0, chosen)
            gate = jnp.where(hit, neg_inf, gate)
        rowbias_ref[h, :nblk] = jnp.where(chosen > 0.0, slope2(h) * KEY_BLOCK * (blk_id - own), neg_inf)
        rowbias_ref[h, nblk:] = jnp.full((TAIL_ROWS, tq), neg_inf, F32)
        qhs.append(qh)

    def position(t):
        t = jnp.asarray(t, jnp.int32)
        is_own = t == 0
        kidx = jnp.where(is_own, qi, jnp.minimum(t - 1, jnp.maximum(qi - 1, 0)))
        ridx = jnp.where(is_own, qi, jnp.where(t <= qi, t - 1, nblk))
        return is_own.astype(jnp.int32), kidx, ridx

    def score_stage(i, slot):
        maxes = []
        for g in range(MOBA_UNROLL):
            kind, kidx, ridx = position(i * MOBA_UNROLL + g)
            for h in heads:
                s = _dot(_key_block(k_ref, kidx, h), qhs[h]) + tile_ref[h, kind]
                s_ref[slot, g * HEADS_PER_STEP + h] = s
                maxes.append(jnp.max(s, axis=0, keepdims=True) + rowbias_ref[h, pl.ds(ridx, 1), :])
        return tuple(maxes)

    def value_stage(i, slot, maxes, states):
        new_states = []
        for h in heads:
            m, acc = states[h]
            m_new = m
            for g in range(MOBA_UNROLL):
                m_new = jnp.maximum(m_new, maxes[g * HEADS_PER_STEP + h])
            acc = jnp.exp2(m - m_new) * acc
            for g in range(MOBA_UNROLL):
                _, kidx, ridx = position(i * MOBA_UNROLL + g)
                shift = m_new - rowbias_ref[h, pl.ds(ridx, 1), :]
                p = jnp.exp2(s_ref[slot, g * HEADS_PER_STEP + h] - shift)
                acc = acc + _dot(_value_block(vt_ref, kidx, h, V_ROWS), p.astype(BF16))
            new_states.append((m_new, acc))
        return tuple(new_states)

    def body(ii, carry):
        maxes0, states = carry
        i = 2 * ii
        maxes1 = score_stage(i + 1, 1)
        states = value_stage(i, 0, maxes0, states)
        maxes2 = score_stage(i + 2, 0)
        states = value_stage(i + 1, 1, maxes1, states)
        return maxes2, states

    n_steps = (qi + MOBA_UNROLL) // MOBA_UNROLL
    n_pairs = (n_steps - 1) // 2
    init = tuple((jnp.full((1, tq), neg_inf, F32), jnp.zeros((V_ROWS, tq), F32)) for _ in heads)
    maxes0, states = lax.fori_loop(0, n_pairs, body, (score_stage(0, 0), init))
    last = 2 * n_pairs

    def two_steps(maxes0, states):
        maxes1 = score_stage(last + 1, 1)
        states = value_stage(last, 0, maxes0, states)
        return value_stage(last + 1, 1, maxes1, states)

    def one_step(maxes0, states):
        return value_stage(last, 0, maxes0, states)

    states = lax.cond(last + 1 < n_steps, two_steps, one_step, maxes0, states)
    outs = [acc[:HEAD_DIM] / acc[HEAD_DIM:HEAD_DIM + 1] for _, acc in states]

    y_ref[0] = jnp.concatenate(outs, axis=0).T.astype(BF16)


def _attention_specs(b, t, v_rows_per_head=HEAD_DIM):
    nblk = t // KEY_BLOCK
    width = HEADS_PER_STEP * HEAD_DIM
    v_rows = HEADS_PER_STEP * v_rows_per_head
    in_specs = [
        pl.BlockSpec((1, width, Q_TILE), lambda bi, hp, qi: (bi, hp, qi)),
        pl.BlockSpec((1, t, width), lambda bi, hp, qi: (bi, 0, hp)),
        pl.BlockSpec((1, nblk, v_rows, KEY_BLOCK), lambda bi, hp, qi: (bi, 0, hp, 0)),
    ]
    out_spec = pl.BlockSpec((1, Q_TILE, width), lambda bi, hp, qi: (bi, qi, hp))
    grid = (b, N_HEADS // HEADS_PER_STEP, t // Q_TILE)
    return grid, in_specs, out_spec


def _moba_attention(slopes, qt, k, vt):
    b, t, _ = k.shape
    nblk = t // KEY_BLOCK
    grid, in_specs, out_spec = _attention_specs(b, t, V_ROWS)
    return pl.pallas_call(
        functools.partial(_moba_kernel, nblk=nblk),
        grid=grid,
        in_specs=[pl.BlockSpec(memory_space=pltpu.SMEM)] + in_specs,
        out_specs=out_spec,
        out_shape=jax.ShapeDtypeStruct((b, t, W_ATT), BF16),
        scratch_shapes=[pltpu.VMEM((nblk, HEADS_PER_STEP * HEAD_DIM), F32),
                        pltpu.VMEM((HEADS_PER_STEP, nblk + TAIL_ROWS, Q_TILE), F32),
                        pltpu.VMEM((HEADS_PER_STEP, 2, KEY_BLOCK, Q_TILE), F32),
                        pltpu.VMEM((2, MOBA_UNROLL * HEADS_PER_STEP, KEY_BLOCK, Q_TILE), F32)],
        compiler_params=pltpu.CompilerParams(
            dimension_semantics=("parallel", "parallel", "arbitrary"), vmem_limit_bytes=VMEM_LIMIT),
        name="moba_attention",
    )(slopes, qt, k, vt)


def _softplus2(z2):
    return jnp.maximum(z2, 0.0) + jnp.log2(1.0 + jnp.exp2(-jnp.abs(z2)))


def _suffix_sums(later_keys, x):
    return _dot(later_keys, x.astype(BF16))


def _sb_kernel(qt_ref, k_ref, vt_ref, y_ref):
    qi = pl.program_id(2)
    tq = Q_TILE
    qt = qt_ref[0]
    key_in_blk = lax.broadcasted_iota(jnp.int32, (KEY_BLOCK, tq), 0)
    qry_in_blk = lax.broadcasted_iota(jnp.int32, (KEY_BLOCK, tq), 1)
    strict = key_in_blk < qry_in_blk
    r = lax.broadcasted_iota(jnp.int32, (KEY_BLOCK, KEY_BLOCK), 0)
    c = lax.broadcasted_iota(jnp.int32, (KEY_BLOCK, KEY_BLOCK), 1)
    later_keys = jnp.where(c > r, 1.0, 0.0).astype(BF16)

    heads = range(HEADS_PER_STEP)
    qhs = [_head_queries(qt, h) for h in heads]

    has_prev = qi > 0
    prev = jnp.maximum(qi - 1, 0)
    z_own = [_dot(_key_block(k_ref, qi, h), qhs[h]) for h in heads]
    z_prev = [_dot(_key_block(k_ref, prev, h), qhs[h]) for h in heads]
    sp_own = [jnp.where(strict, _softplus2(z_own[h]), 0.0) for h in heads]
    sp_prev = [_softplus2(z_prev[h]) for h in heads]
    tail_own = [_suffix_sums(later_keys, sp_own[h]) for h in heads]
    tail_prev = [_suffix_sums(later_keys, sp_prev[h]) for h in heads]
    accs, carries = [], []
    for h in heads:
        w_own = jnp.where(strict, jnp.exp2(z_own[h] - sp_own[h] - tail_own[h]), 0.0)
        carry = jnp.where(has_prev, tail_own[h][0:1, :] + sp_own[h][0:1, :], jnp.inf)
        w_prev = jnp.exp2(z_prev[h] - sp_prev[h] - tail_prev[h] - carry)
        accs.append(_dot(_value_block(vt_ref, qi, h), w_own.astype(BF16))
                    + _dot(_value_block(vt_ref, prev, h), w_prev.astype(BF16)))
        carries.append(carry + tail_prev[h][0:1, :] + sp_prev[h][0:1, :])

    def carry_min(carries):
        return jnp.min(functools.reduce(jnp.minimum, carries))

    def cond(state):
        j, cmin, _, _ = state
        return (j >= 0) & (cmin < SB_LOG2_WEIGHT_FLOOR)

    def body(state):
        j, _, carries, accs = state
        new_carries, new_accs = [], []
        for h in heads:
            z2 = _dot(_key_block(k_ref, j, h), qhs[h])
            sp = _softplus2(z2)
            tail = _suffix_sums(later_keys, sp) + carries[h]
            w = jnp.exp2(z2 - sp - tail)
            new_accs.append(accs[h] + _dot(_value_block(vt_ref, j, h), w.astype(BF16)))
            new_carries.append(tail[0:1, :] + sp[0:1, :])
        return j - 1, carry_min(new_carries), tuple(new_carries), tuple(new_accs)

    _, _, _, accs = lax.while_loop(cond, body, (qi - 2, carry_min(carries), tuple(carries), tuple(accs)))

    y_ref[0] = jnp.concatenate(accs, axis=0).T.astype(BF16)


def _sb_attention(qt, k, vt):
    b, t, _ = k.shape
    grid, in_specs, out_spec = _attention_specs(b, t)
    return pl.pallas_call(
        _sb_kernel,
        grid=grid,
        in_specs=in_specs,
        out_specs=out_spec,
        out_shape=jax.ShapeDtypeStruct((b, t, W_ATT), BF16),
        compiler_params=pltpu.CompilerParams(
            dimension_semantics=("parallel", "parallel", "arbitrary"), vmem_limit_bytes=VMEM_LIMIT),
        name="stickbreak_attention",
    )(qt, k, vt)


def _post_kernel(x_ref, ya_ref, yb_ref, gate_ref, wa_ref, wb_ref, wo_ref, g_ref, wup_ref, cw_ref, cb_ref,
                 wdown_ref, gfin_ref, o_ref, tail_ref, act_ref, *, tiles_per_seq, final_norm):
    tm, d = x_ref.shape
    d_ff = wdown_ref.shape[0]

    @pl.when(pl.program_id(0) % tiles_per_seq == 0)
    def _():
        tail_ref[...] = jnp.zeros_like(tail_ref)

    ga = gate_ref[:, :d].astype(F32)
    gb = gate_ref[:, d:].astype(F32)
    mixed = ga * _dot(ya_ref[...], wa_ref[...]) + gb * _dot(yb_ref[...], wb_ref[...])
    x = x_ref[...] + _dot(mixed.astype(BF16), wo_ref[...])
    o_ref[...] = x
    h = _rms_normalize(x, g_ref[...]).astype(BF16)
    row = lax.broadcasted_iota(jnp.int32, (tm, FF_CHUNK), 0)

    def conv_chunk(lo):
        u = _dot(h, wup_ref[:, lo:lo + FF_CHUNK])
        prev = tail_ref[:, lo:lo + FF_CHUNK]
        tail_ref[:, lo:lo + FF_CHUNK] = u[tm - TAIL_ROWS:, :]
        p1 = prev[TAIL_ROWS - 1:TAIL_ROWS, :]
        p2 = prev[TAIL_ROWS - 2:TAIL_ROWS - 1, :]
        u1 = jnp.where(row == 0, p1, pltpu.roll(u, 1, 0))
        u2 = jnp.where(row == 0, p2, jnp.where(row == 1, p1, pltpu.roll(u, 2, 0)))
        cw = cw_ref[:, lo:lo + FF_CHUNK]
        return cw[0:1] * u2 + cw[1:2] * u1 + cw[2:3] * u + cb_ref[:, lo:lo + FF_CHUNK]

    for c in range(d_ff // FF_CHUNK):
        a = conv_chunk(c * FF_CHUNK)
        bgate = conv_chunk(d_ff + c * FF_CHUNK)
        act_ref[:, c * FF_CHUNK:(c + 1) * FF_CHUNK] = (a * jax.nn.sigmoid(a) * bgate).astype(BF16)

    y = o_ref[...] + _dot(act_ref[...], wdown_ref[...])
    if final_norm:
        y = _rms_normalize(y, gfin_ref[...])
    o_ref[...] = y


def _post(x2, ya, yb, gate, wa, wb, wo, g, wup, cw, cb, wdown, gfin, *, seq_len, final_norm):
    n, d = x2.shape
    d_ff = wdown.shape[0]
    tm = ROW_TILE
    row = lambda width: pl.BlockSpec((tm, width), lambda i: (i, 0))
    full = lambda a: pl.BlockSpec(a.shape, lambda i: (0, 0), pipeline_mode=pl.Buffered(1))
    return pl.pallas_call(
        functools.partial(_post_kernel, tiles_per_seq=seq_len // tm, final_norm=final_norm),
        grid=(n // tm,),
        in_specs=[row(d), row(W_ATT), row(W_ATT), row(2 * d), full(wa), full(wb), full(wo),
                  full(g), full(wup), full(cw), full(cb), full(wdown), full(gfin)],
        out_specs=row(d),
        out_shape=jax.ShapeDtypeStruct((n, d), F32),
        scratch_shapes=[pltpu.VMEM((TAIL_ROWS, 2 * d_ff), F32), pltpu.VMEM((tm, d_ff), BF16)],
        compiler_params=pltpu.CompilerParams(
            dimension_semantics=("arbitrary",), vmem_limit_bytes=VMEM_LIMIT),
        name="mix_conv_ffn",
    )(x2, ya, yb, gate, wa, wb, wo, g, wup, cw, cb, wdown, gfin)


def kernel(x, g_mix, w_in, w_proj_moba, w_proj_sb, w_out, g_ffn, w_up, conv_w, conv_b, w_down, g_final):
    b, t, d = x.shape
    depth = w_in.shape[0]
    assert t % ROW_TILE == 0 and ROW_TILE % KEY_BLOCK == 0 and Q_TILE == KEY_BLOCK
    assert w_up.shape[2] % (2 * FF_CHUNK) == 0 and conv_w.shape[1] == CONV_WIDTH
    slopes = jnp.exp2(-8.0 * jnp.arange(1, N_HEADS + 1, dtype=F32) / N_HEADS)
    gfin = g_final.reshape(1, d)
    for layer in range(depth):
        qta, ka, vta, qtb, kb, vtb, gate = _inproj(x, g_mix[layer], w_in[layer].astype(BF16))
        ya = _moba_attention(slopes, qta, ka, vta)
        yb = _sb_attention(qtb, kb, vtb)
        x = _post(x.reshape(b * t, d), ya.reshape(b * t, W_ATT), yb.reshape(b * t, W_ATT),
                  gate.reshape(b * t, 2 * d), w_proj_moba[layer].astype(BF16),
                  w_proj_sb[layer].astype(BF16), w_out[layer].astype(BF16),
                  g_ffn[layer].reshape(1, d), w_up[layer].astype(BF16),
                  conv_w[layer].reshape(CONV_WIDTH, -1), conv_b[layer].reshape(1, -1),
                  w_down[layer].astype(BF16), gfin, seq_len=t,
                  final_norm=(layer == depth - 1)).reshape(b, t, d)
    return x
```

```python
import functools

import jax
import jax.numpy as jnp
from jax import lax
from jax.experimental import pallas as pl
from jax.experimental.pallas import tpu as pltpu

F32 = jnp.float32
BF16 = jnp.bfloat16

HEAD_DIM = 64
N_HEADS = 8
W_ATT = N_HEADS * HEAD_DIM
LANES = 128
HEADS_PER_GROUP = LANES // HEAD_DIM
HEADS_PER_STEP = 4
KEY_BLOCK = 256
Q_TILE = 256
MOBA_TOPK = 3
ONES_ROWS = 16
V_ROWS = HEAD_DIM + ONES_ROWS
MOBA_UNROLL = 1
LOG2E = 1.4426950408889634
RMS_EPS = 1e-6
CONV_WIDTH = 3
FF_CHUNK = 256
ROW_TILE = 512
TAIL_ROWS = 8
SB_LOG2_WEIGHT_FLOOR = 160.0
VMEM_LIMIT = 56 * 1024 * 1024


def _dot(a, b):
    return jnp.dot(a, b, preferred_element_type=F32)


def _rms_normalize(x, g):
    return x * lax.rsqrt(jnp.mean(x * x, axis=-1, keepdims=True) + RMS_EPS) * g


def _inproj_kernel(x_ref, g_ref, w_ref, qa_ref, ka_ref, va_ref, qb_ref, kb_ref, vb_ref, gate_ref):
    h = _rms_normalize(x_ref[0], g_ref[...]).astype(BF16)
    scale = HEAD_DIM ** -0.5

    def proj(lo, hi):
        return _dot(h, w_ref[:, lo:hi])

    def put_blocks(ref, t):
        for j in range(t.shape[1] // KEY_BLOCK):
            ref[0, j] = t[:, j * KEY_BLOCK:(j + 1) * KEY_BLOCK].astype(BF16)

    def with_ones_rows(t):
        ones = jnp.ones((ONES_ROWS, t.shape[1]), F32)
        parts = []
        for hd in range(N_HEADS):
            parts += [t[hd * HEAD_DIM:(hd + 1) * HEAD_DIM], ones]
        return jnp.concatenate(parts, axis=0)

    w = W_ATT
    gate_ref[0] = jax.nn.sigmoid(proj(6 * w, w_ref.shape[1])).astype(BF16)
    put_blocks(va_ref, with_ones_rows(proj(2 * w, 3 * w).T))
    put_blocks(vb_ref, proj(5 * w, 6 * w).T)
    qa_ref[0] = (proj(0, w) * (scale * LOG2E)).T.astype(BF16)
    qb_ref[0] = (proj(3 * w, 4 * w) * (scale * LOG2E)).T.astype(BF16)
    ka_ref[0] = proj(w, 2 * w).astype(BF16)
    kb_ref[0] = proj(4 * w, 5 * w).astype(BF16)


def _inproj(x, g, w_in):
    b, t, d = x.shape
    d_in = w_in.shape[1]
    tm = ROW_TILE
    nblk = t // KEY_BLOCK
    qt_shape = jax.ShapeDtypeStruct((b, W_ATT, t), BF16)
    k_shape = jax.ShapeDtypeStruct((b, t, W_ATT), BF16)
    gate_shape = jax.ShapeDtypeStruct((b, t, d_in - 6 * W_ATT), BF16)
    qt_spec = pl.BlockSpec((1, W_ATT, tm), lambda bi, i: (bi, 0, i))
    k_spec = pl.BlockSpec((1, tm, W_ATT), lambda bi, i: (bi, i, 0))

    def vt(rows_per_head):
        rows = N_HEADS * rows_per_head
        return (jax.ShapeDtypeStruct((b, nblk, rows, KEY_BLOCK), BF16),
                pl.BlockSpec((1, tm // KEY_BLOCK, rows, KEY_BLOCK), lambda bi, i: (bi, i, 0, 0)))

    vta_shape, vta_spec = vt(HEAD_DIM + ONES_ROWS)
    vtb_shape, vtb_spec = vt(HEAD_DIM)
    return pl.pallas_call(
        _inproj_kernel,
        grid=(b, t // tm),
        in_specs=[
            pl.BlockSpec((1, tm, d), lambda bi, i: (bi, i, 0)),
            pl.BlockSpec((1, d), lambda bi, i: (0, 0)),
            pl.BlockSpec((d, d_in), lambda bi, i: (0, 0)),
        ],
        out_specs=[qt_spec, k_spec, vta_spec, qt_spec, k_spec, vtb_spec,
                   pl.BlockSpec((1, tm, d_in - 6 * W_ATT), lambda bi, i: (bi, i, 0))],
        out_shape=[qt_shape, k_shape, vta_shape, qt_shape, k_shape, vtb_shape, gate_shape],
        compiler_params=pltpu.CompilerParams(
            dimension_semantics=("parallel", "parallel"), vmem_limit_bytes=VMEM_LIMIT),
        name="inproj",
    )(x, g.reshape(1, d), w_in)


def _lane_group(h):
    g = h // HEADS_PER_GROUP
    return slice(g * LANES, (g + 1) * LANES)


def _head_queries(qt, h):
    grp = qt[_lane_group(h), :].astype(F32)
    row = lax.broadcasted_iota(jnp.int32, grp.shape, 0)
    lo = (h % HEADS_PER_GROUP) * HEAD_DIM
    keep = (row >= lo) & (row < lo + HEAD_DIM)
    return jnp.where(keep, grp, 0.0).astype(BF16)


def _key_block(k_ref, j, h):
    rows = pl.ds(pl.multiple_of(j * KEY_BLOCK, KEY_BLOCK), KEY_BLOCK)
    return k_ref[0, rows, _lane_group(h)]


def _value_block(vt_ref, j, h, rows_per_head=HEAD_DIM):
    return vt_ref[0, j, h * rows_per_head:(h + 1) * rows_per_head, :]


def _moba_kernel(slopes_ref, qt_ref, k_ref, vt_ref, y_ref, kmean_ref, rowbias_ref, tile_ref, s_ref, *, nblk):
    hp = pl.program_id(1)
    qi = pl.program_id(2)
    tq = Q_TILE

    heads = range(HEADS_PER_STEP)
    neg_inf = -jnp.inf

    def slope2(h):
        return slopes_ref[hp * HEADS_PER_STEP + h] * LOG2E

    @pl.when(qi == 0)
    def _():
        def body(j, c):
            rows = pl.ds(pl.multiple_of(j * KEY_BLOCK, KEY_BLOCK), KEY_BLOCK)
            blk = k_ref[0, rows, :].astype(F32)
            kmean_ref[pl.ds(j, 1), :] = jnp.sum(blk, axis=0, keepdims=True) * (1.0 / KEY_BLOCK)
            return c
        lax.fori_loop(0, nblk, body, 0)
        key_in_blk = lax.broadcasted_iota(jnp.int32, (KEY_BLOCK, tq), 0)
        qry_in_blk = lax.broadcasted_iota(jnp.int32, (KEY_BLOCK, tq), 1)
        rel = (key_in_blk - qry_in_blk).astype(F32)
        for h in heads:
            alibi = slope2(h) * rel
            tile_ref[h, 0] = alibi
            tile_ref[h, 1] = jnp.where(key_in_blk <= qry_in_blk, alibi, neg_inf)

    km = kmean_ref[...]
    km_hi = km.astype(BF16)
    km_r = km - km_hi.astype(F32)
    km_mid = km_r.astype(BF16)
    km_lo = (km_r - km_mid.astype(F32)).astype(BF16)

    qt = qt_ref[0]
    blk_id = lax.broadcasted_iota(jnp.int32, (nblk, tq), 0).astype(F32)
    own = qi.astype(F32)

    qhs = []
    for h in heads:
        qh = _head_queries(qt, h)
        grp = _lane_group(h)
        gate = _dot(km_hi[:, grp], qh) + _dot(km_mid[:, grp], qh) + _dot(km_lo[:, grp], qh)
        gate = jnp.where(blk_id < own, gate, neg_inf)
        chosen = jnp.where(blk_id == own, 1.0, 0.0)
        for _ in range(min(MOBA_TOPK, nblk)):
            best = jnp.max(gate, axis=0, keepdims=True)
            first = jnp.min(jnp.where(gate == best, blk_id, float(nblk)), axis=0, keepdims=True)
            hit = blk_id == first
            chosen = jnp.where(hit & (best > neg_inf), 1.0, chosen)
            gate = jnp.where(hit, neg_inf, gate)
        rowbias_ref[h, :nblk] = jnp.where(chosen > 0.0, slope2(h) * KEY_BLOCK * (blk_id - own), neg_inf)
        rowbias_ref[h, nblk:] = jnp.full((TAIL_ROWS, tq), neg_inf, F32)
        qhs.append(qh)

    def position(t):
        t = jnp.asarray(t, jnp.int32)
        is_own = t == 0
        kidx = jnp.where(is_own, qi, jnp.minimum(t - 1, jnp.maximum(qi - 1, 0)))
        ridx = jnp.where(is_own, qi, jnp.where(t <= qi, t - 1, nblk))
        return is_own.astype(jnp.int32), kidx, ridx

    def score_stage(i, slot):
        maxes = []
        for g in range(MOBA_UNROLL):
            kind, kidx, ridx = position(i * MOBA_UNROLL + g)
            for h in heads:
                s = _dot(_key_block(k_ref, kidx, h), qhs[h]) + tile_ref[h, kind]
                s_ref[slot, g * HEADS_PER_STEP + h] = s
                maxes.append(jnp.max(s, axis=0, keepdims=True) + rowbias_ref[h, pl.ds(ridx, 1), :])
        return tuple(maxes)

    def value_stage(i, slot, maxes, states):
        new_states = []
        for h in heads:
            m, acc = states[h]
            m_new = m
            for g in range(MOBA_UNROLL):
                m_new = jnp.maximum(m_new, maxes[g * HEADS_PER_STEP + h])
            acc = jnp.exp2(m - m_new) * acc
            for g in range(MOBA_UNROLL):
                _, kidx, ridx = position(i * MOBA_UNROLL + g)
                shift = m_new - rowbias_ref[h, pl.ds(ridx, 1), :]
                p = jnp.exp2(s_ref[slot, g * HEADS_PER_STEP + h] - shift)
                acc = acc + _dot(_value_block(vt_ref, kidx, h, V_ROWS), p.astype(BF16))
            new_states.append((m_new, acc))
        return tuple(new_states)

    def body(ii, carry):
        maxes0, states = carry
        i = 2 * ii
        maxes1 = score_stage(i + 1, 1)
        states = value_stage(i, 0, maxes0, states)
        maxes2 = score_stage(i + 2, 0)
        states = value_stage(i + 1, 1, maxes1, states)
        return maxes2, states

    n_steps = (qi + MOBA_UNROLL) // MOBA_UNROLL
    n_pairs = (n_steps - 1) // 2
    init = tuple((jnp.full((1, tq), neg_inf, F32), jnp.zeros((V_ROWS, tq), F32)) for _ in heads)
    maxes0, states = lax.fori_loop(0, n_pairs, body, (score_stage(0, 0), init))
    last = 2 * n_pairs

    def two_steps(maxes0, states):
        maxes1 = score_stage(last + 1, 1)
        states = value_stage(last, 0, maxes0, states)
        return value_stage(last + 1, 1, maxes1, states)

    def one_step(maxes0, states):
        return value_stage(last, 0, maxes0, states)

    states = lax.cond(last + 1 < n_steps, two_steps, one_step, maxes0, states)
    outs = [acc[:HEAD_DIM] / acc[HEAD_DIM:HEAD_DIM + 1] for _, acc in states]

    y_ref[0] = jnp.concatenate(outs, axis=0).T.astype(BF16)


def _attention_specs(b, t, v_rows_per_head=HEAD_DIM):
    nblk = t // KEY_BLOCK
    width = HEADS_PER_STEP * HEAD_DIM
    v_rows = HEADS_PER_STEP * v_rows_per_head
    in_specs = [
        pl.BlockSpec((1, width, Q_TILE), lambda bi, hp, qi: (bi, hp, qi)),
        pl.BlockSpec((1, t, width), lambda bi, hp, qi: (bi, 0, hp)),
        pl.BlockSpec((1, nblk, v_rows, KEY_BLOCK), lambda bi, hp, qi: (bi, 0, hp, 0)),
    ]
    out_spec = pl.BlockSpec((1, Q_TILE, width), lambda bi, hp, qi: (bi, qi, hp))
    grid = (b, N_HEADS // HEADS_PER_STEP, t // Q_TILE)
    return grid, in_specs, out_spec


def _moba_attention(slopes, qt, k, vt):
    b, t, _ = k.shape
    nblk = t // KEY_BLOCK
    grid, in_specs, out_spec = _attention_specs(b, t, V_ROWS)
    return pl.pallas_call(
        functools.partial(_moba_kernel, nblk=nblk),
        grid=grid,
        in_specs=[pl.BlockSpec(memory_space=pltpu.SMEM)] + in_specs,
        out_specs=out_spec,
        out_shape=jax.ShapeDtypeStruct((b, t, W_ATT), BF16),
        scratch_shapes=[pltpu.VMEM((nblk, HEADS_PER_STEP * HEAD_DIM), F32),
                        pltpu.VMEM((HEADS_PER_STEP, nblk + TAIL_ROWS, Q_TILE), F32),
                        pltpu.VMEM((HEADS_PER_STEP, 2, KEY_BLOCK, Q_TILE), F32),
                        pltpu.VMEM((2, MOBA_UNROLL * HEADS_PER_STEP, KEY_BLOCK, Q_TILE), F32)],
        compiler_params=pltpu.CompilerParams(
            dimension_semantics=("parallel", "parallel", "arbitrary"), vmem_limit_bytes=VMEM_LIMIT),
        name="moba_attention",
    )(slopes, qt, k, vt)


def _softplus2(z2):
    return jnp.maximum(z2, 0.0) + jnp.log2(1.0 + jnp.exp2(-jnp.abs(z2)))


def _sb_terms(z2, mask=None):
    sp = _softplus2(z2)
    if mask is not None:
        sp = jnp.where(mask, sp, 0.0)
    return z2 - sp, sp.astype(BF16), sp[0:1, :]


def _sb_kernel(qt_ref, k_ref, vt_ref, y_ref):
    qi = pl.program_id(2)
    tq = Q_TILE
    qt = qt_ref[0]
    key_in_blk = lax.broadcasted_iota(jnp.int32, (KEY_BLOCK, tq), 0)
    qry_in_blk = lax.broadcasted_iota(jnp.int32, (KEY_BLOCK, tq), 1)
    strict = key_in_blk < qry_in_blk
    r = lax.broadcasted_iota(jnp.int32, (KEY_BLOCK, KEY_BLOCK), 0)
    c = lax.broadcasted_iota(jnp.int32, (KEY_BLOCK, KEY_BLOCK), 1)
    later_keys = jnp.where(c > r, 1.0, 0.0).astype(BF16)

    heads = range(HEADS_PER_STEP)
    qhs = [_head_queries(qt, h) for h in heads]

    has_prev = qi > 0
    prev = jnp.maximum(qi - 1, 0)
    z_own = [_dot(_key_block(k_ref, qi, h), qhs[h]) for h in heads]
    z_prev = [_dot(_key_block(k_ref, prev, h), qhs[h]) for h in heads]
    terms_own = [_sb_terms(z_own[h], strict) for h in heads]
    terms_prev = [_sb_terms(z_prev[h]) for h in heads]
    tail_own = [_dot(later_keys, terms_own[h][1]) for h in heads]
    tail_prev = [_dot(later_keys, terms_prev[h][1]) for h in heads]
    accs, carries = [], []
    for h in heads:
        (logb_own, _, sp0_own), (logb_prev, _, sp0_prev) = terms_own[h], terms_prev[h]
        w_own = jnp.where(strict, jnp.exp2(logb_own - tail_own[h]), 0.0)
        carry = jnp.where(has_prev, tail_own[h][0:1, :] + sp0_own, jnp.inf)
        w_prev = jnp.exp2(logb_prev - tail_prev[h] - carry)
        accs.append(_dot(_value_block(vt_ref, qi, h), w_own.astype(BF16))
                    + _dot(_value_block(vt_ref, prev, h), w_prev.astype(BF16)))
        carries.append(carry + tail_prev[h][0:1, :] + sp0_prev)

    def carry_min(carries):
        return jnp.min(functools.reduce(jnp.minimum, carries))

    def cond(state):
        j, cmin, _, _ = state
        return (j >= 0) & (cmin < SB_LOG2_WEIGHT_FLOOR)

    def body(state):
        j, _, carries, accs = state
        new_carries, new_accs = [], []
        for h in heads:
            logb, sp_b, sp0 = _sb_terms(_dot(_key_block(k_ref, j, h), qhs[h]))
            tail = _dot(later_keys, sp_b) + carries[h]
            w = jnp.exp2(logb - tail)
            new_accs.append(accs[h] + _dot(_value_block(vt_ref, j, h), w.astype(BF16)))
            new_carries.append(tail[0:1, :] + sp0)
        return j - 1, carry_min(new_carries), tuple(new_carries), tuple(new_accs)

    _, _, _, accs = lax.while_loop(cond, body, (qi - 2, carry_min(carries), tuple(carries), tuple(accs)))

    y_ref[0] = jnp.concatenate(accs, axis=0).T.astype(BF16)


def _sb_attention(qt, k, vt):
    b, t, _ = k.shape
    grid, in_specs, out_spec = _attention_specs(b, t)
    return pl.pallas_call(
        _sb_kernel,
        grid=grid,
        in_specs=in_specs,
        out_specs=out_spec,
        out_shape=jax.ShapeDtypeStruct((b, t, W_ATT), BF16),
        compiler_params=pltpu.CompilerParams(
            dimension_semantics=("parallel", "parallel", "arbitrary"), vmem_limit_bytes=VMEM_LIMIT),
        name="stickbreak_attention",
    )(qt, k, vt)


def _post_kernel(x_ref, ya_ref, yb_ref, gate_ref, wa_ref, wb_ref, wo_ref, g_ref, wup_ref, cw_ref, cb_ref,
                 wdown_ref, gfin_ref, o_ref, tail_ref, act_ref, *, tiles_per_seq, final_norm):
    tm, d = x_ref.shape
    d_ff = wdown_ref.shape[0]

    @pl.when(pl.program_id(0) % tiles_per_seq == 0)
    def _():
        tail_ref[...] = jnp.zeros_like(tail_ref)

    ga = gate_ref[:, :d].astype(F32)
    gb = gate_ref[:, d:].astype(F32)
    mixed = ga * _dot(ya_ref[...], wa_ref[...]) + gb * _dot(yb_ref[...], wb_ref[...])
    x = x_ref[...] + _dot(mixed.astype(BF16), wo_ref[...])
    o_ref[...] = x
    h = _rms_normalize(x, g_ref[...]).astype(BF16)
    row = lax.broadcasted_iota(jnp.int32, (tm, FF_CHUNK), 0)

    def conv_chunk(lo):
        u = _dot(h, wup_ref[:, lo:lo + FF_CHUNK])
        prev = tail_ref[:, lo:lo + FF_CHUNK]
        tail_ref[:, lo:lo + FF_CHUNK] = u[tm - TAIL_ROWS:, :]
        p1 = prev[TAIL_ROWS - 1:TAIL_ROWS, :]
        p2 = prev[TAIL_ROWS - 2:TAIL_ROWS - 1, :]
        u1 = jnp.where(row == 0, p1, pltpu.roll(u, 1, 0))
        u2 = jnp.where(row == 0, p2, jnp.where(row == 1, p1, pltpu.roll(u, 2, 0)))
        cw = cw_ref[:, lo:lo + FF_CHUNK]
        return cw[0:1] * u2 + cw[1:2] * u1 + cw[2:3] * u + cb_ref[:, lo:lo + FF_CHUNK]

    for c in range(d_ff // FF_CHUNK):
        a = conv_chunk(c * FF_CHUNK)
        bgate = conv_chunk(d_ff + c * FF_CHUNK)
        act_ref[:, c * FF_CHUNK:(c + 1) * FF_CHUNK] = (a * jax.nn.sigmoid(a) * bgate).astype(BF16)

    y = o_ref[...] + _dot(act_ref[...], wdown_ref[...])
    if final_norm:
        y = _rms_normalize(y, gfin_ref[...])
    o_ref[...] = y


def _post(x2, ya, yb, gate, wa, wb, wo, g, wup, cw, cb, wdown, gfin, *, seq_len, final_norm):
    n, d = x2.shape
    d_ff = wdown.shape[0]
    tm = ROW_TILE
    row = lambda width: pl.BlockSpec((tm, width), lambda i: (i, 0))
    full = lambda a: pl.BlockSpec(a.shape, lambda i: (0, 0), pipeline_mode=pl.Buffered(1))
    return pl.pallas_call(
        functools.partial(_post_kernel, tiles_per_seq=seq_len // tm, final_norm=final_norm),
        grid=(n // tm,),
        in_specs=[row(d), row(W_ATT), row(W_ATT), row(2 * d), full(wa), full(wb), full(wo),
                  full(g), full(wup), full(cw), full(cb), full(wdown), full(gfin)],
        out_specs=row(d),
        out_shape=jax.ShapeDtypeStruct((n, d), F32),
        scratch_shapes=[pltpu.VMEM((TAIL_ROWS, 2 * d_ff), F32), pltpu.VMEM((tm, d_ff), BF16)],
        compiler_params=pltpu.CompilerParams(
            dimension_semantics=("arbitrary",), vmem_limit_bytes=VMEM_LIMIT),
        name="mix_conv_ffn",
    )(x2, ya, yb, gate, wa, wb, wo, g, wup, cw, cb, wdown, gfin)


def kernel(x, g_mix, w_in, w_proj_moba, w_proj_sb, w_out, g_ffn, w_up, conv_w, conv_b, w_down, g_final):
    b, t, d = x.shape
    depth = w_in.shape[0]
    assert t % ROW_TILE == 0 and ROW_TILE % KEY_BLOCK == 0 and Q_TILE == KEY_BLOCK
    assert w_up.shape[2] % (2 * FF_CHUNK) == 0 and conv_w.shape[1] == CONV_WIDTH
    slopes = jnp.exp2(-8.0 * jnp.arange(1, N_HEADS + 1, dtype=F32) / N_HEADS)
    gfin = g_final.reshape(1, d)
    for layer in range(depth):
        qta, ka, vta, qtb, kb, vtb, gate = _inproj(x, g_mix[layer], w_in[layer].astype(BF16))
        ya = _moba_attention(slopes, qta, ka, vta)
        yb = _sb_attention(qtb, kb, vtb)
        x = _post(x.reshape(b * t, d), ya.reshape(b * t, W_ATT), yb.reshape(b * t, W_ATT),
                  gate.reshape(b * t, 2 * d), w_proj_moba[layer].astype(BF16),
                  w_proj_sb[layer].astype(BF16), w_out[layer].astype(BF16),
                  g_ffn[layer].reshape(1, d), w_up[layer].astype(BF16),
                  conv_w[layer].reshape(CONV_WIDTH, -1), conv_b[layer].reshape(1, -1),
                  w_down[layer].astype(BF16), gfin, seq_len=t,
                  final_norm=(layer == depth - 1)).reshape(b, t, d)
    return x
```

```python
import functools

import jax
import jax.numpy as jnp
from jax import lax
from jax.experimental import pallas as pl
from jax.experimental.pallas import tpu as pltpu

F32 = jnp.float32
BF16 = jnp.bfloat16

HEAD_DIM = 64
N_HEADS = 8
W_ATT = N_HEADS * HEAD_DIM
LANES = 128
HEADS_PER_GROUP = LANES // HEAD_DIM
HEADS_PER_STEP = 4
KEY_BLOCK = 256
Q_TILE = 256
MOBA_TOPK = 3
ONES_ROWS = 16
V_ROWS = HEAD_DIM + ONES_ROWS
MOBA_TILES_PER_STEP = 2
LOG2E = 1.4426950408889634
RMS_EPS = 1e-6
CONV_WIDTH = 3
FF_CHUNK = 256
ROW_TILE = 512
TAIL_ROWS = 8
SB_LOG2_WEIGHT_FLOOR = 160.0
VMEM_LIMIT = 56 * 1024 * 1024


def _dot(a, b):
    return jnp.dot(a, b, preferred_element_type=F32)


def _rms_normalize(x, g):
    return x * lax.rsqrt(jnp.mean(x * x, axis=-1, keepdims=True) + RMS_EPS) * g


def _inproj_kernel(x_ref, g_ref, w_ref, qa_ref, ka_ref, va_ref, qb_ref, kb_ref, vb_ref, gate_ref):
    h = _rms_normalize(x_ref[0], g_ref[...]).astype(BF16)
    scale = HEAD_DIM ** -0.5

    def proj(lo, hi):
        return _dot(h, w_ref[:, lo:hi])

    def put_blocks(ref, t):
        for j in range(t.shape[1] // KEY_BLOCK):
            ref[0, j] = t[:, j * KEY_BLOCK:(j + 1) * KEY_BLOCK].astype(BF16)

    def with_ones_rows(t):
        ones = jnp.ones((ONES_ROWS, t.shape[1]), F32)
        parts = []
        for hd in range(N_HEADS):
            parts += [t[hd * HEAD_DIM:(hd + 1) * HEAD_DIM], ones]
        return jnp.concatenate(parts, axis=0)

    w = W_ATT
    gate_ref[0] = jax.nn.sigmoid(proj(6 * w, w_ref.shape[1])).astype(BF16)
    put_blocks(va_ref, with_ones_rows(proj(2 * w, 3 * w).T))
    put_blocks(vb_ref, proj(5 * w, 6 * w).T)
    qa_ref[0] = (proj(0, w) * (scale * LOG2E)).T.astype(BF16)
    qb_ref[0] = (proj(3 * w, 4 * w) * (scale * LOG2E)).T.astype(BF16)
    ka_ref[0] = proj(w, 2 * w).astype(BF16)
    kb_ref[0] = proj(4 * w, 5 * w).astype(BF16)


def _inproj(x, g, w_in):
    b, t, d = x.shape
    d_in = w_in.shape[1]
    tm = ROW_TILE
    nblk = t // KEY_BLOCK
    qt_shape = jax.ShapeDtypeStruct((b, W_ATT, t), BF16)
    k_shape = jax.ShapeDtypeStruct((b, t, W_ATT), BF16)
    gate_shape = jax.ShapeDtypeStruct((b, t, d_in - 6 * W_ATT), BF16)
    qt_spec = pl.BlockSpec((1, W_ATT, tm), lambda bi, i: (bi, 0, i))
    k_spec = pl.BlockSpec((1, tm, W_ATT), lambda bi, i: (bi, i, 0))

    def vt(rows_per_head):
        rows = N_HEADS * rows_per_head
        return (jax.ShapeDtypeStruct((b, nblk, rows, KEY_BLOCK), BF16),
                pl.BlockSpec((1, tm // KEY_BLOCK, rows, KEY_BLOCK), lambda bi, i: (bi, i, 0, 0)))

    vta_shape, vta_spec = vt(HEAD_DIM + ONES_ROWS)
    vtb_shape, vtb_spec = vt(HEAD_DIM)
    return pl.pallas_call(
        _inproj_kernel,
        grid=(b, t // tm),
        in_specs=[
            pl.BlockSpec((1, tm, d), lambda bi, i: (bi, i, 0)),
            pl.BlockSpec((1, d), lambda bi, i: (0, 0)),
            pl.BlockSpec((d, d_in), lambda bi, i: (0, 0)),
        ],
        out_specs=[qt_spec, k_spec, vta_spec, qt_spec, k_spec, vtb_spec,
                   pl.BlockSpec((1, tm, d_in - 6 * W_ATT), lambda bi, i: (bi, i, 0))],
        out_shape=[qt_shape, k_shape, vta_shape, qt_shape, k_shape, vtb_shape, gate_shape],
        compiler_params=pltpu.CompilerParams(
            dimension_semantics=("parallel", "parallel"), vmem_limit_bytes=VMEM_LIMIT),
        name="inproj",
    )(x, g.reshape(1, d), w_in)


def _lane_group(h):
    g = h // HEADS_PER_GROUP
    return slice(g * LANES, (g + 1) * LANES)


def _head_queries(qt, h):
    grp = qt[_lane_group(h), :].astype(F32)
    row = lax.broadcasted_iota(jnp.int32, grp.shape, 0)
    lo = (h % HEADS_PER_GROUP) * HEAD_DIM
    keep = (row >= lo) & (row < lo + HEAD_DIM)
    return jnp.where(keep, grp, 0.0).astype(BF16)


def _key_block(k_ref, j, h):
    rows = pl.ds(pl.multiple_of(j * KEY_BLOCK, KEY_BLOCK), KEY_BLOCK)
    return k_ref[0, rows, _lane_group(h)]


def _value_block(vt_ref, j, h, rows_per_head=HEAD_DIM):
    return vt_ref[0, j, h * rows_per_head:(h + 1) * rows_per_head, :]


def _moba_kernel(slopes_ref, qt_ref, k_ref, vt_ref, y_ref, kmean_ref, rowbias_ref, tile_ref, s_ref, qh_ref,
                 *, nblk):
    hp = pl.program_id(1)
    qa = MOBA_TILES_PER_STEP * pl.program_id(2)
    tq = Q_TILE

    heads = range(HEADS_PER_STEP)
    n_heads = HEADS_PER_STEP
    neg_inf = -jnp.inf

    def slope2(h):
        return slopes_ref[hp * HEADS_PER_STEP + h] * LOG2E

    @pl.when(qa == 0)
    def _():
        def body(j, c):
            rows = pl.ds(pl.multiple_of(j * KEY_BLOCK, KEY_BLOCK), KEY_BLOCK)
            blk = k_ref[0, rows, :].astype(F32)
            kmean_ref[pl.ds(j, 1), :] = jnp.sum(blk, axis=0, keepdims=True) * (1.0 / KEY_BLOCK)
            return c
        lax.fori_loop(0, nblk, body, 0)
        key_in_blk = lax.broadcasted_iota(jnp.int32, (KEY_BLOCK, tq), 0)
        qry_in_blk = lax.broadcasted_iota(jnp.int32, (KEY_BLOCK, tq), 1)
        rel = (key_in_blk - qry_in_blk).astype(F32)
        for h in heads:
            alibi = slope2(h) * rel
            tile_ref[h, 0] = alibi
            tile_ref[h, 1] = jnp.where(key_in_blk <= qry_in_blk, alibi, neg_inf)

    km = kmean_ref[...]
    km_hi = km.astype(BF16)
    km_r = km - km_hi.astype(F32)
    km_mid = km_r.astype(BF16)
    km_lo = (km_r - km_mid.astype(F32)).astype(BF16)

    blk_id = lax.broadcasted_iota(jnp.int32, (nblk, tq), 0).astype(F32)
    for tile in range(MOBA_TILES_PER_STEP):
        qt = qt_ref[0, :, tile * tq:(tile + 1) * tq]
        own = (qa + tile).astype(F32)
        for h in heads:
            qh = _head_queries(qt, h)
            grp = _lane_group(h)
            gate = _dot(km_hi[:, grp], qh) + _dot(km_mid[:, grp], qh) + _dot(km_lo[:, grp], qh)
            gate = jnp.where(blk_id < own, gate, neg_inf)
            chosen = jnp.where(blk_id == own, 1.0, 0.0)
            for _ in range(min(MOBA_TOPK, nblk)):
                best = jnp.max(gate, axis=0, keepdims=True)
                first = jnp.min(jnp.where(gate == best, blk_id, float(nblk)), axis=0, keepdims=True)
                hit = blk_id == first
                chosen = jnp.where(hit & (best > neg_inf), 1.0, chosen)
                gate = jnp.where(hit, neg_inf, gate)
            rowbias_ref[tile * n_heads + h] = jnp.where(
                chosen > 0.0, slope2(h) * KEY_BLOCK * (blk_id - own), neg_inf)
            qh_ref[tile * n_heads + h] = qh

    n_pos_a = qa + 1

    def position(i):
        i = jnp.asarray(i, jnp.int32)
        in_b = (i >= n_pos_a).astype(jnp.int32)
        t = i - in_b * n_pos_a
        is_own = t == 0
        blk = jnp.where(is_own, qa + in_b, t - 1)
        return in_b * n_heads, is_own.astype(jnp.int32), blk

    def score_stage(i, slot):
        base, kind, blk = position(i)
        maxes = []
        for h in heads:
            s = _dot(_key_block(k_ref, blk, h), qh_ref[base + h]) + tile_ref[h, kind]
            s_ref[slot, h] = s
            maxes.append(jnp.max(s, axis=0, keepdims=True) + rowbias_ref[base + h, pl.ds(blk, 1), :])
        return tuple(maxes)

    def value_stage(i, slot, maxes, states):
        base, _, blk = position(i)
        new_states = []
        for h in heads:
            m, acc = states[h]
            m_new = jnp.maximum(m, maxes[h])
            shift = m_new - rowbias_ref[base + h, pl.ds(blk, 1), :]
            p = jnp.exp2(s_ref[slot, h] - shift)
            acc = jnp.exp2(m - m_new) * acc + _dot(_value_block(vt_ref, blk, h, V_ROWS), p.astype(BF16))
            new_states.append((m_new, acc))
        return tuple(new_states)

    def finish_tile(tile, states):
        outs = [acc[:HEAD_DIM] / acc[HEAD_DIM:HEAD_DIM + 1] for _, acc in states]
        y_ref[0, tile * tq:(tile + 1) * tq] = jnp.concatenate(outs, axis=0).T.astype(BF16)

    def body(ii, carry, hand_over=False):
        maxes0, states = carry
        i = 2 * ii
        maxes1 = score_stage(i + 1, 1)
        states = value_stage(i, 0, maxes0, states)
        if hand_over:
            finish_tile(0, states)
            states = init
        maxes2 = score_stage(i + 2, 0)
        states = value_stage(i + 1, 1, maxes1, states)
        return maxes2, states

    init = tuple((jnp.full((1, tq), neg_inf, F32), jnp.zeros((V_ROWS, tq), F32)) for _ in heads)
    mid = qa // 2
    carry = lax.fori_loop(0, mid, body, (score_stage(0, 0), init))
    carry = body(mid, carry, hand_over=True)
    maxes0, states = lax.fori_loop(mid + 1, n_pos_a, body, carry)
    finish_tile(1, value_stage(2 * n_pos_a, 0, maxes0, states))


def _attention_specs(b, t, v_rows_per_head=HEAD_DIM, tiles_per_step=1):
    nblk = t // KEY_BLOCK
    width = HEADS_PER_STEP * HEAD_DIM
    v_rows = HEADS_PER_STEP * v_rows_per_head
    tq = tiles_per_step * Q_TILE
    in_specs = [
        pl.BlockSpec((1, width, tq), lambda bi, hp, qi: (bi, hp, qi)),
        pl.BlockSpec((1, t, width), lambda bi, hp, qi: (bi, 0, hp)),
        pl.BlockSpec((1, nblk, v_rows, KEY_BLOCK), lambda bi, hp, qi: (bi, 0, hp, 0)),
    ]
    out_spec = pl.BlockSpec((1, tq, width), lambda bi, hp, qi: (bi, qi, hp))
    grid = (b, N_HEADS // HEADS_PER_STEP, t // tq)
    return grid, in_specs, out_spec


def _moba_attention(slopes, qt, k, vt):
    b, t, _ = k.shape
    nblk = t // KEY_BLOCK
    grid, in_specs, out_spec = _attention_specs(b, t, V_ROWS, MOBA_TILES_PER_STEP)
    chains = MOBA_TILES_PER_STEP * HEADS_PER_STEP
    return pl.pallas_call(
        functools.partial(_moba_kernel, nblk=nblk),
        grid=grid,
        in_specs=[pl.BlockSpec(memory_space=pltpu.SMEM)] + in_specs,
        out_specs=out_spec,
        out_shape=jax.ShapeDtypeStruct((b, t, W_ATT), BF16),
        scratch_shapes=[pltpu.VMEM((nblk, HEADS_PER_STEP * HEAD_DIM), F32),
                        pltpu.VMEM((chains, nblk, Q_TILE), F32),
                        pltpu.VMEM((HEADS_PER_STEP, 2, KEY_BLOCK, Q_TILE), F32),
                        pltpu.VMEM((2, HEADS_PER_STEP, KEY_BLOCK, Q_TILE), F32),
                        pltpu.VMEM((chains, LANES, Q_TILE), BF16)],
        compiler_params=pltpu.CompilerParams(
            dimension_semantics=("parallel", "parallel", "arbitrary"), vmem_limit_bytes=VMEM_LIMIT),
        name="moba_attention",
    )(slopes, qt, k, vt)


def _softplus2(z2):
    return jnp.maximum(z2, 0.0) + jnp.log2(1.0 + jnp.exp2(-jnp.abs(z2)))


def _sb_terms(z2, mask=None):
    sp = _softplus2(z2)
    if mask is not None:
        sp = jnp.where(mask, sp, 0.0)
    return z2 - sp, sp.astype(BF16), sp[0:1, :]


def _sb_kernel(qt_ref, k_ref, vt_ref, y_ref):
    qi = pl.program_id(2)
    tq = Q_TILE
    qt = qt_ref[0]
    key_in_blk = lax.broadcasted_iota(jnp.int32, (KEY_BLOCK, tq), 0)
    qry_in_blk = lax.broadcasted_iota(jnp.int32, (KEY_BLOCK, tq), 1)
    strict = key_in_blk < qry_in_blk
    r = lax.broadcasted_iota(jnp.int32, (KEY_BLOCK, KEY_BLOCK), 0)
    c = lax.broadcasted_iota(jnp.int32, (KEY_BLOCK, KEY_BLOCK), 1)
    later_keys = jnp.where(c > r, 1.0, 0.0).astype(BF16)

    heads = range(HEADS_PER_STEP)
    qhs = [_head_queries(qt, h) for h in heads]

    has_prev = qi > 0
    prev = jnp.maximum(qi - 1, 0)
    z_own = [_dot(_key_block(k_ref, qi, h), qhs[h]) for h in heads]
    z_prev = [_dot(_key_block(k_ref, prev, h), qhs[h]) for h in heads]
    terms_own = [_sb_terms(z_own[h], strict) for h in heads]
    terms_prev = [_sb_terms(z_prev[h]) for h in heads]
    tail_own = [_dot(later_keys, terms_own[h][1]) for h in heads]
    tail_prev = [_dot(later_keys, terms_prev[h][1]) for h in heads]
    accs, carries = [], []
    for h in heads:
        (logb_own, _, sp0_own), (logb_prev, _, sp0_prev) = terms_own[h], terms_prev[h]
        w_own = jnp.where(strict, jnp.exp2(logb_own - tail_own[h]), 0.0)
        carry = jnp.where(has_prev, tail_own[h][0:1, :] + sp0_own, jnp.inf)
        w_prev = jnp.exp2(logb_prev - tail_prev[h] - carry)
        accs.append(_dot(_value_block(vt_ref, qi, h), w_own.astype(BF16))
                    + _dot(_value_block(vt_ref, prev, h), w_prev.astype(BF16)))
        carries.append(carry + tail_prev[h][0:1, :] + sp0_prev)

    def carry_min(carries):
        return jnp.min(functools.reduce(jnp.minimum, carries))

    def cond(state):
        j, cmin, _, _ = state
        return (j >= 0) & (cmin < SB_LOG2_WEIGHT_FLOOR)

    def body(state):
        j, _, carries, accs = state
        new_carries, new_accs = [], []
        for h in heads:
            logb, sp_b, sp0 = _sb_terms(_dot(_key_block(k_ref, j, h), qhs[h]))
            tail = _dot(later_keys, sp_b) + carries[h]
            w = jnp.exp2(logb - tail)
            new_accs.append(accs[h] + _dot(_value_block(vt_ref, j, h), w.astype(BF16)))
            new_carries.append(tail[0:1, :] + sp0)
        return j - 1, carry_min(new_carries), tuple(new_carries), tuple(new_accs)

    _, _, _, accs = lax.while_loop(cond, body, (qi - 2, carry_min(carries), tuple(carries), tuple(accs)))

    y_ref[0] = jnp.concatenate(accs, axis=0).T.astype(BF16)


def _sb_attention(qt, k, vt):
    b, t, _ = k.shape
    grid, in_specs, out_spec = _attention_specs(b, t)
    return pl.pallas_call(
        _sb_kernel,
        grid=grid,
        in_specs=in_specs,
        out_specs=out_spec,
        out_shape=jax.ShapeDtypeStruct((b, t, W_ATT), BF16),
        compiler_params=pltpu.CompilerParams(
            dimension_semantics=("parallel", "parallel", "arbitrary"), vmem_limit_bytes=VMEM_LIMIT),
        name="stickbreak_attention",
    )(qt, k, vt)


def _post_kernel(x_ref, ya_ref, yb_ref, gate_ref, wa_ref, wb_ref, wo_ref, g_ref, wup_ref, cw_ref, cb_ref,
                 wdown_ref, gfin_ref, o_ref, tail_ref, act_ref, *, tiles_per_seq, final_norm):
    tm, d = x_ref.shape
    d_ff = wdown_ref.shape[0]

    @pl.when(pl.program_id(0) % tiles_per_seq == 0)
    def _():
        tail_ref[...] = jnp.zeros_like(tail_ref)

    ga = gate_ref[:, :d].astype(F32)
    gb = gate_ref[:, d:].astype(F32)
    mixed = ga * _dot(ya_ref[...], wa_ref[...]) + gb * _dot(yb_ref[...], wb_ref[...])
    x = x_ref[...] + _dot(mixed.astype(BF16), wo_ref[...])
    o_ref[...] = x
    h = _rms_normalize(x, g_ref[...]).astype(BF16)
    row = lax.broadcasted_iota(jnp.int32, (tm, FF_CHUNK), 0)

    def conv_chunk(lo):
        u = _dot(h, wup_ref[:, lo:lo + FF_CHUNK])
        prev = tail_ref[:, lo:lo + FF_CHUNK]
        tail_ref[:, lo:lo + FF_CHUNK] = u[tm - TAIL_ROWS:, :]
        p1 = prev[TAIL_ROWS - 1:TAIL_ROWS, :]
        p2 = prev[TAIL_ROWS - 2:TAIL_ROWS - 1, :]
        u1 = jnp.where(row == 0, p1, pltpu.roll(u, 1, 0))
        u2 = jnp.where(row == 0, p2, jnp.where(row == 1, p1, pltpu.roll(u, 2, 0)))
        cw = cw_ref[:, lo:lo + FF_CHUNK]
        return cw[0:1] * u2 + cw[1:2] * u1 + cw[2:3] * u + cb_ref[:, lo:lo + FF_CHUNK]

    for c in range(d_ff // FF_CHUNK):
        a = conv_chunk(c * FF_CHUNK)
        bgate = conv_chunk(d_ff + c * FF_CHUNK)
        act_ref[:, c * FF_CHUNK:(c + 1) * FF_CHUNK] = (a * jax.nn.sigmoid(a) * bgate).astype(BF16)

    y = o_ref[...] + _dot(act_ref[...], wdown_ref[...])
    if final_norm:
        y = _rms_normalize(y, gfin_ref[...])
    o_ref[...] = y


def _post(x2, ya, yb, gate, wa, wb, wo, g, wup, cw, cb, wdown, gfin, *, seq_len, final_norm):
    n, d = x2.shape
    d_ff = wdown.shape[0]
    tm = ROW_TILE
    row = lambda width: pl.BlockSpec((tm, width), lambda i: (i, 0))
    full = lambda a: pl.BlockSpec(a.shape, lambda i: (0, 0), pipeline_mode=pl.Buffered(1))
    return pl.pallas_call(
        functools.partial(_post_kernel, tiles_per_seq=seq_len // tm, final_norm=final_norm),
        grid=(n // tm,),
        in_specs=[row(d), row(W_ATT), row(W_ATT), row(2 * d), full(wa), full(wb), full(wo),
                  full(g), full(wup), full(cw), full(cb), full(wdown), full(gfin)],
        out_specs=row(d),
        out_shape=jax.ShapeDtypeStruct((n, d), F32),
        scratch_shapes=[pltpu.VMEM((TAIL_ROWS, 2 * d_ff), F32), pltpu.VMEM((tm, d_ff), BF16)],
        compiler_params=pltpu.CompilerParams(
            dimension_semantics=("arbitrary",), vmem_limit_bytes=VMEM_LIMIT),
        name="mix_conv_ffn",
    )(x2, ya, yb, gate, wa, wb, wo, g, wup, cw, cb, wdown, gfin)


def kernel(x, g_mix, w_in, w_proj_moba, w_proj_sb, w_out, g_ffn, w_up, conv_w, conv_b, w_down, g_final):
    b, t, d = x.shape
    depth = w_in.shape[0]
    assert t % ROW_TILE == 0 and ROW_TILE % KEY_BLOCK == 0 and Q_TILE == KEY_BLOCK
    assert w_up.shape[2] % (2 * FF_CHUNK) == 0 and conv_w.shape[1] == CONV_WIDTH
    slopes = jnp.exp2(-8.0 * jnp.arange(1, N_HEADS + 1, dtype=F32) / N_HEADS)
    gfin = g_final.reshape(1, d)
    for layer in range(depth):
        qta, ka, vta, qtb, kb, vtb, gate = _inproj(x, g_mix[layer], w_in[layer].astype(BF16))
        ya = _moba_attention(slopes, qta, ka, vta)
        yb = _sb_attention(qtb, kb, vtb)
        x = _post(x.reshape(b * t, d), ya.reshape(b * t, W_ATT), yb.reshape(b * t, W_ATT),
                  gate.reshape(b * t, 2 * d), w_proj_moba[layer].astype(BF16),
                  w_proj_sb[layer].astype(BF16), w_out[layer].astype(BF16),
                  g_ffn[layer].reshape(1, d), w_up[layer].astype(BF16),
                  conv_w[layer].reshape(CONV_WIDTH, -1), conv_b[layer].reshape(1, -1),
                  w_down[layer].astype(BF16), gfin, seq_len=t,
                  final_norm=(layer == depth - 1)).reshape(b, t, d)
    return x
```

```python
import functools

import jax
import jax.numpy as jnp
from jax import lax
from jax.experimental import pallas as pl
from jax.experimental.pallas import tpu as pltpu

F32 = jnp.float32
BF16 = jnp.bfloat16

HEAD_DIM = 64
N_HEADS = 8
W_ATT = N_HEADS * HEAD_DIM
LANES = 128
HEADS_PER_GROUP = LANES // HEAD_DIM
HEADS_PER_STEP = 4
KEY_BLOCK = 256
Q_TILE = 256
MOBA_TOPK = 3
ONES_ROWS = 16
V_ROWS = HEAD_DIM + ONES_ROWS
MOBA_TILES_PER_STEP = 2
LOG2E = 1.4426950408889634
RMS_EPS = 1e-6
CONV_WIDTH = 3
FF_CHUNK = 256
ROW_TILE = 512
TAIL_ROWS = 8
SB_LOG2_WEIGHT_FLOOR = 160.0
VMEM_LIMIT = 56 * 1024 * 1024


def _dot(a, b):
    return jnp.dot(a, b, preferred_element_type=F32)


def _rms_normalize(x, g):
    return x * lax.rsqrt(jnp.mean(x * x, axis=-1, keepdims=True) + RMS_EPS) * g


def _inproj_kernel(x_ref, g_ref, w_ref, qa_ref, ka_ref, va_ref, qb_ref, kb_ref, vb_ref, gate_ref):
    h = _rms_normalize(x_ref[0], g_ref[...]).astype(BF16)
    scale = HEAD_DIM ** -0.5

    def proj(lo, hi):
        return _dot(h, w_ref[:, lo:hi])

    def put_blocks(ref, t):
        for j in range(t.shape[1] // KEY_BLOCK):
            ref[0, j] = t[:, j * KEY_BLOCK:(j + 1) * KEY_BLOCK].astype(BF16)

    def with_ones_rows(t):
        ones = jnp.ones((ONES_ROWS, t.shape[1]), F32)
        parts = []
        for hd in range(N_HEADS):
            parts += [t[hd * HEAD_DIM:(hd + 1) * HEAD_DIM], ones]
        return jnp.concatenate(parts, axis=0)

    w = W_ATT
    gate_ref[0] = jax.nn.sigmoid(proj(6 * w, w_ref.shape[1])).astype(BF16)
    put_blocks(va_ref, with_ones_rows(proj(2 * w, 3 * w).T))
    put_blocks(vb_ref, proj(5 * w, 6 * w).T)
    qa_ref[0] = (proj(0, w) * (scale * LOG2E)).T.astype(BF16)
    qb_ref[0] = (proj(3 * w, 4 * w) * (scale * LOG2E)).T.astype(BF16)
    ka_ref[0] = proj(w, 2 * w).astype(BF16)
    kb_ref[0] = proj(4 * w, 5 * w).astype(BF16)


def _inproj(x, g, w_in):
    b, t, d = x.shape
    d_in = w_in.shape[1]
    tm = ROW_TILE
    nblk = t // KEY_BLOCK
    qt_shape = jax.ShapeDtypeStruct((b, W_ATT, t), BF16)
    k_shape = jax.ShapeDtypeStruct((b, t, W_ATT), BF16)
    gate_shape = jax.ShapeDtypeStruct((b, t, d_in - 6 * W_ATT), BF16)
    qt_spec = pl.BlockSpec((1, W_ATT, tm), lambda bi, i: (bi, 0, i))
    k_spec = pl.BlockSpec((1, tm, W_ATT), lambda bi, i: (bi, i, 0))

    def vt(rows_per_head):
        rows = N_HEADS * rows_per_head
        return (jax.ShapeDtypeStruct((b, nblk, rows, KEY_BLOCK), BF16),
                pl.BlockSpec((1, tm // KEY_BLOCK, rows, KEY_BLOCK), lambda bi, i: (bi, i, 0, 0)))

    vta_shape, vta_spec = vt(HEAD_DIM + ONES_ROWS)
    vtb_shape, vtb_spec = vt(HEAD_DIM)
    return pl.pallas_call(
        _inproj_kernel,
        grid=(b, t // tm),
        in_specs=[
            pl.BlockSpec((1, tm, d), lambda bi, i: (bi, i, 0)),
            pl.BlockSpec((1, d), lambda bi, i: (0, 0)),
            pl.BlockSpec((d, d_in), lambda bi, i: (0, 0)),
        ],
        out_specs=[qt_spec, k_spec, vta_spec, qt_spec, k_spec, vtb_spec,
                   pl.BlockSpec((1, tm, d_in - 6 * W_ATT), lambda bi, i: (bi, i, 0))],
        out_shape=[qt_shape, k_shape, vta_shape, qt_shape, k_shape, vtb_shape, gate_shape],
        compiler_params=pltpu.CompilerParams(
            dimension_semantics=("parallel", "parallel"), vmem_limit_bytes=VMEM_LIMIT),
        name="inproj",
    )(x, g.reshape(1, d), w_in)


def _lane_group(h):
    g = h // HEADS_PER_GROUP
    return slice(g * LANES, (g + 1) * LANES)


def _head_queries(qt, h):
    grp = qt[_lane_group(h), :].astype(F32)
    row = lax.broadcasted_iota(jnp.int32, grp.shape, 0)
    lo = (h % HEADS_PER_GROUP) * HEAD_DIM
    keep = (row >= lo) & (row < lo + HEAD_DIM)
    return jnp.where(keep, grp, 0.0).astype(BF16)


def _key_block(k_ref, j, h):
    rows = pl.ds(pl.multiple_of(j * KEY_BLOCK, KEY_BLOCK), KEY_BLOCK)
    return k_ref[0, rows, _lane_group(h)]


def _value_block(vt_ref, j, h, rows_per_head=HEAD_DIM):
    return vt_ref[0, j, h * rows_per_head:(h + 1) * rows_per_head, :]


def _moba_kernel(slopes_ref, qt_ref, k_ref, vt_ref, y_ref, kmean_ref, rowbias_ref, tile_ref, s_ref, *, nblk):
    hp = pl.program_id(1)
    qa = MOBA_TILES_PER_STEP * pl.program_id(2)
    tq = Q_TILE

    heads = range(HEADS_PER_STEP)
    n_heads = HEADS_PER_STEP
    neg_inf = -jnp.inf

    def slope2(h):
        return slopes_ref[hp * HEADS_PER_STEP + h] * LOG2E

    @pl.when(qa == 0)
    def _():
        def body(j, c):
            rows = pl.ds(pl.multiple_of(j * KEY_BLOCK, KEY_BLOCK), KEY_BLOCK)
            blk = k_ref[0, rows, :].astype(F32)
            kmean_ref[pl.ds(j, 1), :] = jnp.sum(blk, axis=0, keepdims=True) * (1.0 / KEY_BLOCK)
            return c
        lax.fori_loop(0, nblk, body, 0)
        key_in_blk = lax.broadcasted_iota(jnp.int32, (KEY_BLOCK, tq), 0)
        qry_in_blk = lax.broadcasted_iota(jnp.int32, (KEY_BLOCK, tq), 1)
        rel = (key_in_blk - qry_in_blk).astype(F32)
        for h in heads:
            alibi = slope2(h) * rel
            tile_ref[h, 0] = alibi
            tile_ref[h, 1] = jnp.where(key_in_blk <= qry_in_blk, alibi, neg_inf)

    km = kmean_ref[...]
    km_hi = km.astype(BF16)
    km_r = km - km_hi.astype(F32)
    km_mid = km_r.astype(BF16)
    km_lo = (km_r - km_mid.astype(F32)).astype(BF16)

    blk_id = lax.broadcasted_iota(jnp.int32, (nblk, tq), 0).astype(F32)
    qhs = [[] for _ in range(MOBA_TILES_PER_STEP)]
    for tile in range(MOBA_TILES_PER_STEP):
        qt = qt_ref[0, :, tile * tq:(tile + 1) * tq]
        own = (qa + tile).astype(F32)
        for h in heads:
            qh = _head_queries(qt, h)
            grp = _lane_group(h)
            gate = _dot(km_hi[:, grp], qh) + _dot(km_mid[:, grp], qh) + _dot(km_lo[:, grp], qh)
            gate = jnp.where(blk_id < own, gate, neg_inf)
            chosen = jnp.where(blk_id == own, 1.0, 0.0)
            for _ in range(min(MOBA_TOPK, nblk)):
                best = jnp.max(gate, axis=0, keepdims=True)
                first = jnp.min(jnp.where(gate == best, blk_id, float(nblk)), axis=0, keepdims=True)
                hit = blk_id == first
                chosen = jnp.where(hit & (best > neg_inf), 1.0, chosen)
                gate = jnp.where(hit, neg_inf, gate)
            rowbias_ref[tile * n_heads + h] = jnp.where(
                chosen > 0.0, slope2(h) * KEY_BLOCK * (blk_id - own), neg_inf)
            qhs[tile].append(qh)

    n_pos_a = qa + 1

    def position(i, tile):
        t = jnp.asarray(i, jnp.int32) - tile * n_pos_a
        is_own = t == 0
        blk = jnp.where(is_own, qa + tile, t - 1)
        return is_own.astype(jnp.int32), blk

    def score_stage(i, tile, slot):
        kind, blk = position(i, tile)
        maxes = []
        for h in heads:
            s = _dot(_key_block(k_ref, blk, h), qhs[tile][h]) + tile_ref[h, kind]
            s_ref[slot, h] = s
            maxes.append(jnp.max(s, axis=0, keepdims=True) + rowbias_ref[tile * n_heads + h, pl.ds(blk, 1), :])
        return tuple(maxes)

    def value_stage(i, tile, slot, maxes, states):
        _, blk = position(i, tile)
        new_states = []
        for h in heads:
            m, acc = states[h]
            m_new = jnp.maximum(m, maxes[h])
            shift = m_new - rowbias_ref[tile * n_heads + h, pl.ds(blk, 1), :]
            p = jnp.exp2(s_ref[slot, h] - shift)
            acc = jnp.exp2(m - m_new) * acc + _dot(_value_block(vt_ref, blk, h, V_ROWS), p.astype(BF16))
            new_states.append((m_new, acc))
        return tuple(new_states)

    def finish_tile(tile, states):
        outs = [acc[:HEAD_DIM] / acc[HEAD_DIM:HEAD_DIM + 1] for _, acc in states]
        y_ref[0, tile * tq:(tile + 1) * tq] = jnp.concatenate(outs, axis=0).T.astype(BF16)

    def body(ii, carry, tile=0, hand_over=False):
        maxes0, states = carry
        i = 2 * ii
        next_tile = 1 if hand_over else tile
        maxes1 = score_stage(i + 1, next_tile, 1)
        states = value_stage(i, tile, 0, maxes0, states)
        if hand_over:
            finish_tile(tile, states)
            states = init
        maxes2 = score_stage(i + 2, next_tile, 0)
        states = value_stage(i + 1, next_tile, 1, maxes1, states)
        return maxes2, states

    init = tuple((jnp.full((1, tq), neg_inf, F32), jnp.zeros((V_ROWS, tq), F32)) for _ in heads)
    mid = qa // 2
    carry = lax.fori_loop(0, mid, functools.partial(body, tile=0), (score_stage(0, 0, 0), init))
    carry = body(mid, carry, tile=0, hand_over=True)
    maxes0, states = lax.fori_loop(mid + 1, n_pos_a, functools.partial(body, tile=1), carry)
    finish_tile(1, value_stage(2 * n_pos_a, 1, 0, maxes0, states))


def _attention_specs(b, t, v_rows_per_head=HEAD_DIM, tiles_per_step=1):
    nblk = t // KEY_BLOCK
    width = HEADS_PER_STEP * HEAD_DIM
    v_rows = HEADS_PER_STEP * v_rows_per_head
    tq = tiles_per_step * Q_TILE
    in_specs = [
        pl.BlockSpec((1, width, tq), lambda bi, hp, qi: (bi, hp, qi)),
        pl.BlockSpec((1, t, width), lambda bi, hp, qi: (bi, 0, hp)),
        pl.BlockSpec((1, nblk, v_rows, KEY_BLOCK), lambda bi, hp, qi: (bi, 0, hp, 0)),
    ]
    out_spec = pl.BlockSpec((1, tq, width), lambda bi, hp, qi: (bi, qi, hp))
    grid = (b, N_HEADS // HEADS_PER_STEP, t // tq)
    return grid, in_specs, out_spec


def _moba_attention(slopes, qt, k, vt):
    b, t, _ = k.shape
    nblk = t // KEY_BLOCK
    grid, in_specs, out_spec = _attention_specs(b, t, V_ROWS, MOBA_TILES_PER_STEP)
    chains = MOBA_TILES_PER_STEP * HEADS_PER_STEP
    return pl.pallas_call(
        functools.partial(_moba_kernel, nblk=nblk),
        grid=grid,
        in_specs=[pl.BlockSpec(memory_space=pltpu.SMEM)] + in_specs,
        out_specs=out_spec,
        out_shape=jax.ShapeDtypeStruct((b, t, W_ATT), BF16),
        scratch_shapes=[pltpu.VMEM((nblk, HEADS_PER_STEP * HEAD_DIM), F32),
                        pltpu.VMEM((chains, nblk, Q_TILE), F32),
                        pltpu.VMEM((HEADS_PER_STEP, 2, KEY_BLOCK, Q_TILE), F32),
                        pltpu.VMEM((2, HEADS_PER_STEP, KEY_BLOCK, Q_TILE), F32)],
        compiler_params=pltpu.CompilerParams(
            dimension_semantics=("parallel", "parallel", "arbitrary"), vmem_limit_bytes=VMEM_LIMIT),
        name="moba_attention",
    )(slopes, qt, k, vt)


def _softplus2(z2):
    return jnp.maximum(z2, 0.0) + jnp.log2(1.0 + jnp.exp2(-jnp.abs(z2)))


def _sb_terms(z2, mask=None):
    sp = _softplus2(z2)
    if mask is not None:
        sp = jnp.where(mask, sp, 0.0)
    return z2 - sp, sp.astype(BF16), sp[0:1, :]


def _sb_kernel(qt_ref, k_ref, vt_ref, y_ref):
    qi = pl.program_id(2)
    tq = Q_TILE
    qt = qt_ref[0]
    key_in_blk = lax.broadcasted_iota(jnp.int32, (KEY_BLOCK, tq), 0)
    qry_in_blk = lax.broadcasted_iota(jnp.int32, (KEY_BLOCK, tq), 1)
    strict = key_in_blk < qry_in_blk
    r = lax.broadcasted_iota(jnp.int32, (KEY_BLOCK, KEY_BLOCK), 0)
    c = lax.broadcasted_iota(jnp.int32, (KEY_BLOCK, KEY_BLOCK), 1)
    later_keys = jnp.where(c > r, 1.0, 0.0).astype(BF16)

    heads = range(HEADS_PER_STEP)
    qhs = [_head_queries(qt, h) for h in heads]

    has_prev = qi > 0
    prev = jnp.maximum(qi - 1, 0)
    z_own = [_dot(_key_block(k_ref, qi, h), qhs[h]) for h in heads]
    z_prev = [_dot(_key_block(k_ref, prev, h), qhs[h]) for h in heads]
    terms_own = [_sb_terms(z_own[h], strict) for h in heads]
    terms_prev = [_sb_terms(z_prev[h]) for h in heads]
    tail_own = [_dot(later_keys, terms_own[h][1]) for h in heads]
    tail_prev = [_dot(later_keys, terms_prev[h][1]) for h in heads]
    accs, carries = [], []
    for h in heads:
        (logb_own, _, sp0_own), (logb_prev, _, sp0_prev) = terms_own[h], terms_prev[h]
        w_own = jnp.where(strict, jnp.exp2(logb_own - tail_own[h]), 0.0)
        carry = jnp.where(has_prev, tail_own[h][0:1, :] + sp0_own, jnp.inf)
        w_prev = jnp.exp2(logb_prev - tail_prev[h] - carry)
        accs.append(_dot(_value_block(vt_ref, qi, h), w_own.astype(BF16))
                    + _dot(_value_block(vt_ref, prev, h), w_prev.astype(BF16)))
        carries.append(carry + tail_prev[h][0:1, :] + sp0_prev)

    def carry_min(carries):
        return jnp.min(functools.reduce(jnp.minimum, carries))

    def cond(state):
        j, cmin, _, _ = state
        return (j >= 0) & (cmin < SB_LOG2_WEIGHT_FLOOR)

    def body(state):
        j, _, carries, accs = state
        new_carries, new_accs = [], []
        for h in heads:
            logb, sp_b, sp0 = _sb_terms(_dot(_key_block(k_ref, j, h), qhs[h]))
            tail = _dot(later_keys, sp_b) + carries[h]
            w = jnp.exp2(logb - tail)
            new_accs.append(accs[h] + _dot(_value_block(vt_ref, j, h), w.astype(BF16)))
            new_carries.append(tail[0:1, :] + sp0)
        return j - 1, carry_min(new_carries), tuple(new_carries), tuple(new_accs)

    _, _, _, accs = lax.while_loop(cond, body, (qi - 2, carry_min(carries), tuple(carries), tuple(accs)))

    y_ref[0] = jnp.concatenate(accs, axis=0).T.astype(BF16)


def _sb_attention(qt, k, vt):
    b, t, _ = k.shape
    grid, in_specs, out_spec = _attention_specs(b, t)
    return pl.pallas_call(
        _sb_kernel,
        grid=grid,
        in_specs=in_specs,
        out_specs=out_spec,
        out_shape=jax.ShapeDtypeStruct((b, t, W_ATT), BF16),
        compiler_params=pltpu.CompilerParams(
            dimension_semantics=("parallel", "parallel", "arbitrary"), vmem_limit_bytes=VMEM_LIMIT),
        name="stickbreak_attention",
    )(qt, k, vt)


def _post_kernel(x_ref, ya_ref, yb_ref, gate_ref, wa_ref, wb_ref, wo_ref, g_ref, wup_ref, cw_ref, cb_ref,
                 wdown_ref, gfin_ref, o_ref, tail_ref, act_ref, *, tiles_per_seq, final_norm):
    tm, d = x_ref.shape
    d_ff = wdown_ref.shape[0]

    @pl.when(pl.program_id(0) % tiles_per_seq == 0)
    def _():
        tail_ref[...] = jnp.zeros_like(tail_ref)

    ga = gate_ref[:, :d].astype(F32)
    gb = gate_ref[:, d:].astype(F32)
    mixed = ga * _dot(ya_ref[...], wa_ref[...]) + gb * _dot(yb_ref[...], wb_ref[...])
    x = x_ref[...] + _dot(mixed.astype(BF16), wo_ref[...])
    o_ref[...] = x
    h = _rms_normalize(x, g_ref[...]).astype(BF16)
    row = lax.broadcasted_iota(jnp.int32, (tm, FF_CHUNK), 0)

    def conv_chunk(lo):
        u = _dot(h, wup_ref[:, lo:lo + FF_CHUNK])
        prev = tail_ref[:, lo:lo + FF_CHUNK]
        tail_ref[:, lo:lo + FF_CHUNK] = u[tm - TAIL_ROWS:, :]
        p1 = prev[TAIL_ROWS - 1:TAIL_ROWS, :]
        p2 = prev[TAIL_ROWS - 2:TAIL_ROWS - 1, :]
        u1 = jnp.where(row == 0, p1, pltpu.roll(u, 1, 0))
        u2 = jnp.where(row == 0, p2, jnp.where(row == 1, p1, pltpu.roll(u, 2, 0)))
        cw = cw_ref[:, lo:lo + FF_CHUNK]
        return cw[0:1] * u2 + cw[1:2] * u1 + cw[2:3] * u + cb_ref[:, lo:lo + FF_CHUNK]

    for c in range(d_ff // FF_CHUNK):
        a = conv_chunk(c * FF_CHUNK)
        bgate = conv_chunk(d_ff + c * FF_CHUNK)
        act_ref[:, c * FF_CHUNK:(c + 1) * FF_CHUNK] = (a * jax.nn.sigmoid(a) * bgate).astype(BF16)

    y = o_ref[...] + _dot(act_ref[...], wdown_ref[...])
    if final_norm:
        y = _rms_normalize(y, gfin_ref[...])
    o_ref[...] = y


def _post(x2, ya, yb, gate, wa, wb, wo, g, wup, cw, cb, wdown, gfin, *, seq_len, final_norm):
    n, d = x2.shape
    d_ff = wdown.shape[0]
    tm = ROW_TILE
    row = lambda width: pl.BlockSpec((tm, width), lambda i: (i, 0))
    full = lambda a: pl.BlockSpec(a.shape, lambda i: (0, 0), pipeline_mode=pl.Buffered(1))
    return pl.pallas_call(
        functools.partial(_post_kernel, tiles_per_seq=seq_len // tm, final_norm=final_norm),
        grid=(n // tm,),
        in_specs=[row(d), row(W_ATT), row(W_ATT), row(2 * d), full(wa), full(wb), full(wo),
                  full(g), full(wup), full(cw), full(cb), full(wdown), full(gfin)],
        out_specs=row(d),
        out_shape=jax.ShapeDtypeStruct((n, d), F32),
        scratch_shapes=[pltpu.VMEM((TAIL_ROWS, 2 * d_ff), F32), pltpu.VMEM((tm, d_ff), BF16)],
        compiler_params=pltpu.CompilerParams(
            dimension_semantics=("arbitrary",), vmem_limit_bytes=VMEM_LIMIT),
        name="mix_conv_ffn",
    )(x2, ya, yb, gate, wa, wb, wo, g, wup, cw, cb, wdown, gfin)


def kernel(x, g_mix, w_in, w_proj_moba, w_proj_sb, w_out, g_ffn, w_up, conv_w, conv_b, w_down, g_final):
    b, t, d = x.shape
    depth = w_in.shape[0]
    assert t % ROW_TILE == 0 and ROW_TILE % KEY_BLOCK == 0 and Q_TILE == KEY_BLOCK
    assert w_up.shape[2] % (2 * FF_CHUNK) == 0 and conv_w.shape[1] == CONV_WIDTH
    slopes = jnp.exp2(-8.0 * jnp.arange(1, N_HEADS + 1, dtype=F32) / N_HEADS)
    gfin = g_final.reshape(1, d)
    for layer in range(depth):
        qta, ka, vta, qtb, kb, vtb, gate = _inproj(x, g_mix[layer], w_in[layer].astype(BF16))
        ya = _moba_attention(slopes, qta, ka, vta)
        yb = _sb_attention(qtb, kb, vtb)
        x = _post(x.reshape(b * t, d), ya.reshape(b * t, W_ATT), yb.reshape(b * t, W_ATT),
                  gate.reshape(b * t, 2 * d), w_proj_moba[layer].astype(BF16),
                  w_proj_sb[layer].astype(BF16), w_out[layer].astype(BF16),
                  g_ffn[layer].reshape(1, d), w_up[layer].astype(BF16),
                  conv_w[layer].reshape(CONV_WIDTH, -1), conv_b[layer].reshape(1, -1),
                  w_down[layer].astype(BF16), gfin, seq_len=t,
                  final_norm=(layer == depth - 1)).reshape(b, t, d)
    return x
```

```python
import functools

import jax
import jax.numpy as jnp
from jax import lax
from jax.experimental import pallas as pl
from jax.experimental.pallas import tpu as pltpu

F32 = jnp.float32
BF16 = jnp.bfloat16

HEAD_DIM = 64
N_HEADS = 8
W_ATT = N_HEADS * HEAD_DIM
LANES = 128
HEADS_PER_GROUP = LANES // HEAD_DIM
HEADS_PER_STEP = 4
KEY_BLOCK = 256
Q_TILE = 256
MOBA_TOPK = 3
ONES_ROWS = 16
V_ROWS = HEAD_DIM + ONES_ROWS
MOBA_TILES_PER_STEP = 2
LOG2E = 1.4426950408889634
RMS_EPS = 1e-6
CONV_WIDTH = 3
FF_CHUNK = 256
ROW_TILE = 512
TAIL_ROWS = 8
SB_LOG2_WEIGHT_FLOOR = 160.0
VMEM_LIMIT = 56 * 1024 * 1024


def _dot(a, b):
    return jnp.dot(a, b, preferred_element_type=F32)


def _rms_normalize(x, g):
    return x * lax.rsqrt(jnp.mean(x * x, axis=-1, keepdims=True) + RMS_EPS) * g


def _inproj_kernel(x_ref, g_ref, w_ref, qa_ref, ka_ref, va_ref, qb_ref, kb_ref, vb_ref, gate_ref):
    h = _rms_normalize(x_ref[0], g_ref[...]).astype(BF16)
    scale = HEAD_DIM ** -0.5

    def proj(lo, hi):
        return _dot(h, w_ref[:, lo:hi])

    def put_blocks(ref, t):
        for j in range(t.shape[1] // KEY_BLOCK):
            ref[0, j] = t[:, j * KEY_BLOCK:(j + 1) * KEY_BLOCK].astype(BF16)

    def with_ones_rows(t):
        ones = jnp.ones((ONES_ROWS, t.shape[1]), F32)
        parts = []
        for hd in range(N_HEADS):
            parts += [t[hd * HEAD_DIM:(hd + 1) * HEAD_DIM], ones]
        return jnp.concatenate(parts, axis=0)

    w = W_ATT
    gate_ref[0] = jax.nn.sigmoid(proj(6 * w, w_ref.shape[1])).astype(BF16)
    put_blocks(va_ref, with_ones_rows(proj(2 * w, 3 * w).T))
    put_blocks(vb_ref, proj(5 * w, 6 * w).T)
    qa_ref[0] = (proj(0, w) * (scale * LOG2E)).T.astype(BF16)
    qb_ref[0] = (proj(3 * w, 4 * w) * (scale * LOG2E)).T.astype(BF16)
    ka_ref[0] = proj(w, 2 * w).astype(BF16)
    kb_ref[0] = proj(4 * w, 5 * w).astype(BF16)


def _inproj(x, g, w_in):
    b, t, d = x.shape
    d_in = w_in.shape[1]
    tm = ROW_TILE
    nblk = t // KEY_BLOCK
    qt_shape = jax.ShapeDtypeStruct((b, W_ATT, t), BF16)
    k_shape = jax.ShapeDtypeStruct((b, t, W_ATT), BF16)
    gate_shape = jax.ShapeDtypeStruct((b, t, d_in - 6 * W_ATT), BF16)
    qt_spec = pl.BlockSpec((1, W_ATT, tm), lambda bi, i: (bi, 0, i))
    k_spec = pl.BlockSpec((1, tm, W_ATT), lambda bi, i: (bi, i, 0))

    def vt(rows_per_head):
        rows = N_HEADS * rows_per_head
        return (jax.ShapeDtypeStruct((b, nblk, rows, KEY_BLOCK), BF16),
                pl.BlockSpec((1, tm // KEY_BLOCK, rows, KEY_BLOCK), lambda bi, i: (bi, i, 0, 0)))

    vta_shape, vta_spec = vt(HEAD_DIM + ONES_ROWS)
    vtb_shape, vtb_spec = vt(HEAD_DIM)
    return pl.pallas_call(
        _inproj_kernel,
        grid=(b, t // tm),
        in_specs=[
            pl.BlockSpec((1, tm, d), lambda bi, i: (bi, i, 0)),
            pl.BlockSpec((1, d), lambda bi, i: (0, 0)),
            pl.BlockSpec((d, d_in), lambda bi, i: (0, 0)),
        ],
        out_specs=[qt_spec, k_spec, vta_spec, qt_spec, k_spec, vtb_spec,
                   pl.BlockSpec((1, tm, d_in - 6 * W_ATT), lambda bi, i: (bi, i, 0))],
        out_shape=[qt_shape, k_shape, vta_shape, qt_shape, k_shape, vtb_shape, gate_shape],
        compiler_params=pltpu.CompilerParams(
            dimension_semantics=("parallel", "parallel"), vmem_limit_bytes=VMEM_LIMIT),
        name="inproj",
    )(x, g.reshape(1, d), w_in)


def _lane_group(h):
    g = h // HEADS_PER_GROUP
    return slice(g * LANES, (g + 1) * LANES)


def _head_queries(qt, h):
    grp = qt[_lane_group(h), :].astype(F32)
    row = lax.broadcasted_iota(jnp.int32, grp.shape, 0)
    lo = (h % HEADS_PER_GROUP) * HEAD_DIM
    keep = (row >= lo) & (row < lo + HEAD_DIM)
    return jnp.where(keep, grp, 0.0).astype(BF16)


def _key_block(k_ref, j, h):
    rows = pl.ds(pl.multiple_of(j * KEY_BLOCK, KEY_BLOCK), KEY_BLOCK)
    return k_ref[0, rows, _lane_group(h)]


def _value_block(vt_ref, j, h, rows_per_head=HEAD_DIM):
    return vt_ref[0, j, h * rows_per_head:(h + 1) * rows_per_head, :]


def _moba_kernel(slopes_ref, qt_ref, k_ref, vt_ref, y_ref, kmean_ref, rowbias_ref, tile_ref, s_ref, *, nblk):
    hp = pl.program_id(1)
    qa = MOBA_TILES_PER_STEP * pl.program_id(2)
    tq = Q_TILE

    heads = range(HEADS_PER_STEP)
    n_heads = HEADS_PER_STEP
    neg_inf = -jnp.inf

    def slope2(h):
        return slopes_ref[hp * HEADS_PER_STEP + h] * LOG2E

    @pl.when(qa == 0)
    def _():
        def body(j, c):
            rows = pl.ds(pl.multiple_of(j * KEY_BLOCK, KEY_BLOCK), KEY_BLOCK)
            blk = k_ref[0, rows, :].astype(F32)
            kmean_ref[pl.ds(j, 1), :] = jnp.sum(blk, axis=0, keepdims=True) * (1.0 / KEY_BLOCK)
            return c
        lax.fori_loop(0, nblk, body, 0)
        key_in_blk = lax.broadcasted_iota(jnp.int32, (KEY_BLOCK, tq), 0)
        qry_in_blk = lax.broadcasted_iota(jnp.int32, (KEY_BLOCK, tq), 1)
        rel = (key_in_blk - qry_in_blk).astype(F32)
        for h in heads:
            alibi = slope2(h) * rel
            tile_ref[h, 0] = alibi
            tile_ref[h, 1] = jnp.where(key_in_blk <= qry_in_blk, alibi, neg_inf)

    km = kmean_ref[...]
    km_hi = km.astype(BF16)
    km_r = km - km_hi.astype(F32)
    km_mid = km_r.astype(BF16)
    km_lo = (km_r - km_mid.astype(F32)).astype(BF16)

    blk_id = lax.broadcasted_iota(jnp.int32, (nblk, tq), 0).astype(F32)
    qhs = [[] for _ in range(MOBA_TILES_PER_STEP)]
    for tile in range(MOBA_TILES_PER_STEP):
        qt = qt_ref[0, :, tile * tq:(tile + 1) * tq]
        own = (qa + tile).astype(F32)
        for h in heads:
            qh = _head_queries(qt, h)
            grp = _lane_group(h)
            gate = _dot(km_hi[:, grp], qh) + _dot(km_mid[:, grp], qh) + _dot(km_lo[:, grp], qh)
            gate = jnp.where(blk_id < own, gate, neg_inf)
            chosen = jnp.where(blk_id == own, 1.0, 0.0)
            for _ in range(min(MOBA_TOPK, nblk)):
                best = jnp.max(gate, axis=0, keepdims=True)
                first = jnp.min(jnp.where(gate == best, blk_id, float(nblk)), axis=0, keepdims=True)
                hit = blk_id == first
                chosen = jnp.where(hit & (best > neg_inf), 1.0, chosen)
                gate = jnp.where(hit, neg_inf, gate)
            rowbias_ref[tile * n_heads + h] = jnp.where(
                chosen > 0.0, slope2(h) * KEY_BLOCK * (blk_id - own), neg_inf)
            qhs[tile].append(qh)

    n_pos_a = qa + 1

    def position(i, tile):
        t = jnp.asarray(i, jnp.int32) - tile * n_pos_a
        is_own = t == 0
        blk = jnp.where(is_own, qa + tile, t - 1)
        return is_own.astype(jnp.int32), blk

    def score_stage(i, tile, slot):
        kind, blk = position(i, tile)
        maxes = []
        for h in heads:
            s = _dot(_key_block(k_ref, blk, h), qhs[tile][h]) + tile_ref[h, kind]
            s_ref[slot, h] = s
            maxes.append(jnp.max(s, axis=0, keepdims=True) + rowbias_ref[tile * n_heads + h, pl.ds(blk, 1), :])
        return tuple(maxes)

    def value_stage(i, tile, slot, maxes, states):
        _, blk = position(i, tile)
        new_states = []
        for h in heads:
            m, acc = states[h]
            m_new = jnp.maximum(m, maxes[h])
            shift = m_new - rowbias_ref[tile * n_heads + h, pl.ds(blk, 1), :]
            p = jnp.exp2(s_ref[slot, h] - shift)
            acc = jnp.exp2(m - m_new) * acc + _dot(_value_block(vt_ref, blk, h, V_ROWS), p.astype(BF16))
            new_states.append((m_new, acc))
        return tuple(new_states)

    def finish_tile(tile, states):
        outs = [acc[:HEAD_DIM] / acc[HEAD_DIM:HEAD_DIM + 1] for _, acc in states]
        y_ref[0, tile * tq:(tile + 1) * tq] = jnp.concatenate(outs, axis=0).T.astype(BF16)

    def body(ii, carry, tile=0, hand_over=False):
        maxes0, states = carry
        i = 2 * ii
        next_tile = 1 if hand_over else tile
        maxes1 = score_stage(i + 1, next_tile, 1)
        states = value_stage(i, tile, 0, maxes0, states)
        if hand_over:
            finish_tile(tile, states)
            states = init
        maxes2 = score_stage(i + 2, next_tile, 0)
        states = value_stage(i + 1, next_tile, 1, maxes1, states)
        return maxes2, states

    init = tuple((jnp.full((1, tq), neg_inf, F32), jnp.zeros((V_ROWS, tq), F32)) for _ in heads)
    mid = qa // 2
    carry = lax.fori_loop(0, mid, functools.partial(body, tile=0), (score_stage(0, 0, 0), init))
    carry = body(mid, carry, tile=0, hand_over=True)
    maxes0, states = lax.fori_loop(mid + 1, n_pos_a, functools.partial(body, tile=1), carry)
    finish_tile(1, value_stage(2 * n_pos_a, 1, 0, maxes0, states))


def _attention_specs(b, t, v_rows_per_head=HEAD_DIM, tiles_per_step=1):
    nblk = t // KEY_BLOCK
    width = HEADS_PER_STEP * HEAD_DIM
    v_rows = HEADS_PER_STEP * v_rows_per_head
    tq = tiles_per_step * Q_TILE
    in_specs = [
        pl.BlockSpec((1, width, tq), lambda bi, hp, qi: (bi, hp, qi)),
        pl.BlockSpec((1, t, width), lambda bi, hp, qi: (bi, 0, hp)),
        pl.BlockSpec((1, nblk, v_rows, KEY_BLOCK), lambda bi, hp, qi: (bi, 0, hp, 0)),
    ]
    out_spec = pl.BlockSpec((1, tq, width), lambda bi, hp, qi: (bi, qi, hp))
    grid = (b, N_HEADS // HEADS_PER_STEP, t // tq)
    return grid, in_specs, out_spec


def _moba_attention(slopes, qt, k, vt):
    b, t, _ = k.shape
    nblk = t // KEY_BLOCK
    grid, in_specs, out_spec = _attention_specs(b, t, V_ROWS, MOBA_TILES_PER_STEP)
    chains = MOBA_TILES_PER_STEP * HEADS_PER_STEP
    return pl.pallas_call(
        functools.partial(_moba_kernel, nblk=nblk),
        grid=grid,
        in_specs=[pl.BlockSpec(memory_space=pltpu.SMEM)] + in_specs,
        out_specs=out_spec,
        out_shape=jax.ShapeDtypeStruct((b, t, W_ATT), BF16),
        scratch_shapes=[pltpu.VMEM((nblk, HEADS_PER_STEP * HEAD_DIM), F32),
                        pltpu.VMEM((chains, nblk, Q_TILE), F32),
                        pltpu.VMEM((HEADS_PER_STEP, 2, KEY_BLOCK, Q_TILE), F32),
                        pltpu.VMEM((2, HEADS_PER_STEP, KEY_BLOCK, Q_TILE), F32)],
        compiler_params=pltpu.CompilerParams(
            dimension_semantics=("parallel", "parallel", "arbitrary"), vmem_limit_bytes=VMEM_LIMIT),
        name="moba_attention",
    )(slopes, qt, k, vt)


def _softplus2(z2):
    return jnp.maximum(z2, 0.0) + jnp.log2(1.0 + jnp.exp2(-jnp.abs(z2)))


def _sb_terms(z2, mask=None):
    sp = _softplus2(z2)
    if mask is not None:
        sp = jnp.where(mask, sp, 0.0)
    return z2 - sp, sp.astype(BF16), sp[0:1, :]


def _sb_kernel(qt_ref, k_ref, vt_ref, y_ref):
    qi = pl.program_id(2)
    tq = Q_TILE
    qt = qt_ref[0]
    key_in_blk = lax.broadcasted_iota(jnp.int32, (KEY_BLOCK, tq), 0)
    qry_in_blk = lax.broadcasted_iota(jnp.int32, (KEY_BLOCK, tq), 1)
    strict = key_in_blk < qry_in_blk
    r = lax.broadcasted_iota(jnp.int32, (KEY_BLOCK, KEY_BLOCK), 0)
    c = lax.broadcasted_iota(jnp.int32, (KEY_BLOCK, KEY_BLOCK), 1)
    later_keys = jnp.where(c > r, 1.0, 0.0).astype(BF16)

    heads = range(HEADS_PER_STEP)
    qhs = [_head_queries(qt, h) for h in heads]

    has_prev = qi > 0
    prev = jnp.maximum(qi - 1, 0)
    z_own = [_dot(_key_block(k_ref, qi, h), qhs[h]) for h in heads]
    z_prev = [_dot(_key_block(k_ref, prev, h), qhs[h]) for h in heads]
    terms_own = [_sb_terms(z_own[h], strict) for h in heads]
    terms_prev = [_sb_terms(z_prev[h]) for h in heads]
    tail_own = [_dot(later_keys, terms_own[h][1]) for h in heads]
    tail_prev = [_dot(later_keys, terms_prev[h][1]) for h in heads]
    accs, carries = [], []
    for h in heads:
        (logb_own, _, sp0_own), (logb_prev, _, sp0_prev) = terms_own[h], terms_prev[h]
        w_own = jnp.where(strict, jnp.exp2(logb_own - tail_own[h]), 0.0)
        carry = jnp.where(has_prev, tail_own[h][0:1, :] + sp0_own, jnp.inf)
        w_prev = jnp.exp2(logb_prev - tail_prev[h] - carry)
        accs.append(_dot(_value_block(vt_ref, qi, h), w_own.astype(BF16))
                    + _dot(_value_block(vt_ref, prev, h), w_prev.astype(BF16)))
        carries.append(carry + tail_prev[h][0:1, :] + sp0_prev)

    def carry_min(carries):
        return jnp.min(functools.reduce(jnp.minimum, carries))

    def cond(state):
        j, cmin, _, _ = state
        return (j >= 0) & (cmin < SB_LOG2_WEIGHT_FLOOR)

    def body(state):
        j, _, carries, accs = state
        new_carries, new_accs = [], []
        for h in heads:
            logb, sp_b, sp0 = _sb_terms(_dot(_key_block(k_ref, j, h), qhs[h]))
            tail = _dot(later_keys, sp_b) + carries[h]
            w = jnp.exp2(logb - tail)
            new_accs.append(accs[h] + _dot(_value_block(vt_ref, j, h), w.astype(BF16)))
            new_carries.append(tail[0:1, :] + sp0)
        return j - 1, carry_min(new_carries), tuple(new_carries), tuple(new_accs)

    _, _, _, accs = lax.while_loop(cond, body, (qi - 2, carry_min(carries), tuple(carries), tuple(accs)))

    y_ref[0] = jnp.concatenate(accs, axis=0).T.astype(BF16)


def _sb_attention(qt, k, vt):
    b, t, _ = k.shape
    grid, in_specs, out_spec = _attention_specs(b, t)
    return pl.pallas_call(
        _sb_kernel,
        grid=grid,
        in_specs=in_specs,
        out_specs=out_spec,
        out_shape=jax.ShapeDtypeStruct((b, t, W_ATT), BF16),
        compiler_params=pltpu.CompilerParams(
            dimension_semantics=("parallel", "parallel", "arbitrary"), vmem_limit_bytes=VMEM_LIMIT),
        name="stickbreak_attention",
    )(qt, k, vt)


def _post_kernel(x_ref, ya_ref, yb_ref, gate_ref, wa_ref, wb_ref, wo_ref, g_ref, wup_ref, cw_ref, cb_ref,
                 wdown_ref, gfin_ref, o_ref, tail_ref, act_ref, *, tiles_per_seq, final_norm):
    tm, d = x_ref.shape
    d_ff = wdown_ref.shape[0]

    @pl.when(pl.program_id(0) % tiles_per_seq == 0)
    def _():
        tail_ref[...] = jnp.zeros_like(tail_ref)

    ga = gate_ref[:, :d].astype(F32)
    gb = gate_ref[:, d:].astype(F32)
    mixed = ga * _dot(ya_ref[...], wa_ref[...]) + gb * _dot(yb_ref[...], wb_ref[...])
    x = x_ref[...] + _dot(mixed.astype(BF16), wo_ref[...])
    o_ref[...] = x
    h = _rms_normalize(x, g_ref[...]).astype(BF16)
    row = lax.broadcasted_iota(jnp.int32, (tm, FF_CHUNK), 0)

    def conv_chunk(lo):
        u = _dot(h, wup_ref[:, lo:lo + FF_CHUNK])
        prev = tail_ref[:, lo:lo + FF_CHUNK]
        tail_ref[:, lo:lo + FF_CHUNK] = u[tm - TAIL_ROWS:, :]
        p1 = prev[TAIL_ROWS - 1:TAIL_ROWS, :]
        p2 = prev[TAIL_ROWS - 2:TAIL_ROWS - 1, :]
        u1 = jnp.where(row == 0, p1, pltpu.roll(u, 1, 0))
        u2 = jnp.where(row == 0, p2, jnp.where(row == 1, p1, pltpu.roll(u, 2, 0)))
        cw = cw_ref[:, lo:lo + FF_CHUNK]
        return cw[0:1] * u2 + cw[1:2] * u1 + cw[2:3] * u + cb_ref[:, lo:lo + FF_CHUNK]

    for c in range(d_ff // FF_CHUNK):
        a = conv_chunk(c * FF_CHUNK)
        bgate = conv_chunk(d_ff + c * FF_CHUNK)
        act_ref[:, c * FF_CHUNK:(c + 1) * FF_CHUNK] = (a * jax.nn.sigmoid(a) * bgate).astype(BF16)

    y = o_ref[...] + _dot(act_ref[...], wdown_ref[...])
    if final_norm:
        y = _rms_normalize(y, gfin_ref[...])
    o_ref[...] = y


def _post(x2, ya, yb, gate, wa, wb, wo, g, wup, cw, cb, wdown, gfin, *, seq_len, final_norm):
    n, d = x2.shape
    d_ff = wdown.shape[0]
    tm = ROW_TILE
    row = lambda width: pl.BlockSpec((tm, width), lambda i: (i, 0))
    full = lambda a: pl.BlockSpec(a.shape, lambda i: (0, 0), pipeline_mode=pl.Buffered(1))
    return pl.pallas_call(
        functools.partial(_post_kernel, tiles_per_seq=seq_len // tm, final_norm=final_norm),
        grid=(n // tm,),
        in_specs=[row(d), row(W_ATT), row(W_ATT), row(2 * d), full(wa), full(wb), full(wo),
                  full(g), full(wup), full(cw), full(cb), full(wdown), full(gfin)],
        out_specs=row(d),
        out_shape=jax.ShapeDtypeStruct((n, d), F32),
        scratch_shapes=[pltpu.VMEM((TAIL_ROWS, 2 * d_ff), F32), pltpu.VMEM((tm, d_ff), BF16)],
        compiler_params=pltpu.CompilerParams(
            dimension_semantics=("arbitrary",), vmem_limit_bytes=VMEM_LIMIT),
        name="mix_conv_ffn",
    )(x2, ya, yb, gate, wa, wb, wo, g, wup, cw, cb, wdown, gfin)


def kernel(x, g_mix, w_in, w_proj_moba, w_proj_sb, w_out, g_ffn, w_up, conv_w, conv_b, w_down, g_final):
    b, t, d = x.shape
    depth = w_in.shape[0]
    assert t % ROW_TILE == 0 and ROW_TILE % KEY_BLOCK == 0 and Q_TILE == KEY_BLOCK
    assert MOBA_TILES_PER_STEP == 2 and t % (MOBA_TILES_PER_STEP * Q_TILE) == 0
    assert w_up.shape[2] % (2 * FF_CHUNK) == 0 and conv_w.shape[1] == CONV_WIDTH
    slopes = jnp.exp2(-8.0 * jnp.arange(1, N_HEADS + 1, dtype=F32) / N_HEADS)
    gfin = g_final.reshape(1, d)
    for layer in range(depth):
        qta, ka, vta, qtb, kb, vtb, gate = _inproj(x, g_mix[layer], w_in[layer].astype(BF16))
        ya = _moba_attention(slopes, qta, ka, vta)
        yb = _sb_attention(qtb, kb, vtb)
        x = _post(x.reshape(b * t, d), ya.reshape(b * t, W_ATT), yb.reshape(b * t, W_ATT),
                  gate.reshape(b * t, 2 * d), w_proj_moba[layer].astype(BF16),
                  w_proj_sb[layer].astype(BF16), w_out[layer].astype(BF16),
                  g_ffn[layer].reshape(1, d), w_up[layer].astype(BF16),
                  conv_w[layer].reshape(CONV_WIDTH, -1), conv_b[layer].reshape(1, -1),
                  w_down[layer].astype(BF16), gfin, seq_len=t,
                  final_norm=(layer == depth - 1)).reshape(b, t, d)
    return x
```

```python
import functools

import jax
import jax.numpy as jnp
from jax import lax
from jax.experimental import pallas as pl
from jax.experimental.pallas import tpu as pltpu

F32 = jnp.float32
BF16 = jnp.bfloat16

HEAD_DIM = 64
N_HEADS = 8
W_ATT = N_HEADS * HEAD_DIM
LANES = 128
HEADS_PER_GROUP = LANES // HEAD_DIM
HEADS_PER_STEP = 4
KEY_BLOCK = 256
Q_TILE = 256
MOBA_TOPK = 3
ONES_ROWS = 16
V_ROWS = HEAD_DIM + ONES_ROWS
MOBA_TILES_PER_STEP = 2
LOG2E = 1.4426950408889634
RMS_EPS = 1e-6
CONV_WIDTH = 3
FF_CHUNK = 256
ROW_TILE = 512
TAIL_ROWS = 8
SB_LOG2_WEIGHT_FLOOR = 160.0
VMEM_LIMIT = 56 * 1024 * 1024


def _dot(a, b):
    return jnp.dot(a, b, preferred_element_type=F32)


def _rms_normalize(x, g):
    return x * lax.rsqrt(jnp.mean(x * x, axis=-1, keepdims=True) + RMS_EPS) * g


def _inproj_kernel(x_ref, g_ref, w_ref, qa_ref, ka_ref, va_ref, qb_ref, kb_ref, vb_ref, gate_ref):
    h = _rms_normalize(x_ref[0], g_ref[...]).astype(BF16)
    scale = HEAD_DIM ** -0.5

    def proj(lo, hi):
        return _dot(h, w_ref[:, lo:hi])

    def put_blocks(ref, t):
        for j in range(t.shape[1] // KEY_BLOCK):
            ref[0, j] = t[:, j * KEY_BLOCK:(j + 1) * KEY_BLOCK].astype(BF16)

    def with_ones_rows(t):
        ones = jnp.ones((ONES_ROWS, t.shape[1]), F32)
        parts = []
        for hd in range(N_HEADS):
            parts += [t[hd * HEAD_DIM:(hd + 1) * HEAD_DIM], ones]
        return jnp.concatenate(parts, axis=0)

    w = W_ATT
    gate_ref[0] = jax.nn.sigmoid(proj(6 * w, w_ref.shape[1])).astype(BF16)
    put_blocks(va_ref, with_ones_rows(proj(2 * w, 3 * w).T))
    put_blocks(vb_ref, proj(5 * w, 6 * w).T)
    def per_head_slabs(t):
        zeros = jnp.zeros((HEAD_DIM, t.shape[1]), F32)
        parts = []
        for hd in range(N_HEADS):
            rows = t[hd * HEAD_DIM:(hd + 1) * HEAD_DIM]
            parts += [rows, zeros] if hd % HEADS_PER_GROUP == 0 else [zeros, rows]
        return jnp.concatenate(parts, axis=0).astype(BF16)

    qa_ref[0] = per_head_slabs((proj(0, w) * (scale * LOG2E)).T)
    qb_ref[0] = per_head_slabs((proj(3 * w, 4 * w) * (scale * LOG2E)).T)
    ka_ref[0] = proj(w, 2 * w).astype(BF16)
    kb_ref[0] = proj(4 * w, 5 * w).astype(BF16)


def _inproj(x, g, w_in):
    b, t, d = x.shape
    d_in = w_in.shape[1]
    tm = ROW_TILE
    nblk = t // KEY_BLOCK
    qt_shape = jax.ShapeDtypeStruct((b, N_HEADS * LANES, t), BF16)
    k_shape = jax.ShapeDtypeStruct((b, t, W_ATT), BF16)
    gate_shape = jax.ShapeDtypeStruct((b, t, d_in - 6 * W_ATT), BF16)
    qt_spec = pl.BlockSpec((1, N_HEADS * LANES, tm), lambda bi, i: (bi, 0, i))
    k_spec = pl.BlockSpec((1, tm, W_ATT), lambda bi, i: (bi, i, 0))

    def vt(rows_per_head):
        rows = N_HEADS * rows_per_head
        return (jax.ShapeDtypeStruct((b, nblk, rows, KEY_BLOCK), BF16),
                pl.BlockSpec((1, tm // KEY_BLOCK, rows, KEY_BLOCK), lambda bi, i: (bi, i, 0, 0)))

    vta_shape, vta_spec = vt(HEAD_DIM + ONES_ROWS)
    vtb_shape, vtb_spec = vt(HEAD_DIM)
    return pl.pallas_call(
        _inproj_kernel,
        grid=(b, t // tm),
        in_specs=[
            pl.BlockSpec((1, tm, d), lambda bi, i: (bi, i, 0)),
            pl.BlockSpec((1, d), lambda bi, i: (0, 0)),
            pl.BlockSpec((d, d_in), lambda bi, i: (0, 0)),
        ],
        out_specs=[qt_spec, k_spec, vta_spec, qt_spec, k_spec, vtb_spec,
                   pl.BlockSpec((1, tm, d_in - 6 * W_ATT), lambda bi, i: (bi, i, 0))],
        out_shape=[qt_shape, k_shape, vta_shape, qt_shape, k_shape, vtb_shape, gate_shape],
        compiler_params=pltpu.CompilerParams(
            dimension_semantics=("parallel", "parallel"), vmem_limit_bytes=VMEM_LIMIT),
        name="inproj",
    )(x, g.reshape(1, d), w_in)


def _lane_group(h):
    g = h // HEADS_PER_GROUP
    return slice(g * LANES, (g + 1) * LANES)


def _head_queries(qt, h):
    return qt[h * LANES:(h + 1) * LANES, :]


def _key_block(k_ref, j, h):
    rows = pl.ds(pl.multiple_of(j * KEY_BLOCK, KEY_BLOCK), KEY_BLOCK)
    return k_ref[0, rows, _lane_group(h)]


def _value_block(vt_ref, j, h, rows_per_head=HEAD_DIM):
    return vt_ref[0, j, h * rows_per_head:(h + 1) * rows_per_head, :]


def _moba_kernel(slopes_ref, qt_ref, k_ref, vt_ref, y_ref, kmean_ref, rowbias_ref, tile_ref, s_ref, *, nblk):
    hp = pl.program_id(1)
    qa = MOBA_TILES_PER_STEP * pl.program_id(2)
    tq = Q_TILE

    heads = range(HEADS_PER_STEP)
    n_heads = HEADS_PER_STEP
    neg_inf = -jnp.inf

    def slope2(h):
        return slopes_ref[hp * HEADS_PER_STEP + h] * LOG2E

    @pl.when(qa == 0)
    def _():
        def body(j, c):
            rows = pl.ds(pl.multiple_of(j * KEY_BLOCK, KEY_BLOCK), KEY_BLOCK)
            blk = k_ref[0, rows, :].astype(F32)
            kmean_ref[pl.ds(j, 1), :] = jnp.sum(blk, axis=0, keepdims=True) * (1.0 / KEY_BLOCK)
            return c
        lax.fori_loop(0, nblk, body, 0)
        key_in_blk = lax.broadcasted_iota(jnp.int32, (KEY_BLOCK, tq), 0)
        qry_in_blk = lax.broadcasted_iota(jnp.int32, (KEY_BLOCK, tq), 1)
        rel = (key_in_blk - qry_in_blk).astype(F32)
        for h in heads:
            alibi = slope2(h) * rel
            tile_ref[h, 0] = alibi
            tile_ref[h, 1] = jnp.where(key_in_blk <= qry_in_blk, alibi, neg_inf)

    km = kmean_ref[...]
    km_hi = km.astype(BF16)
    km_r = km - km_hi.astype(F32)
    km_mid = km_r.astype(BF16)
    km_lo = (km_r - km_mid.astype(F32)).astype(BF16)

    blk_id = lax.broadcasted_iota(jnp.int32, (nblk, tq), 0).astype(F32)
    for tile in range(MOBA_TILES_PER_STEP):
        qt = qt_ref[0, :, tile * tq:(tile + 1) * tq]
        own = (qa + tile).astype(F32)
        for h in heads:
            qh = _head_queries(qt, h)
            grp = _lane_group(h)
            gate = _dot(km_hi[:, grp], qh) + _dot(km_mid[:, grp], qh) + _dot(km_lo[:, grp], qh)
            gate = jnp.where(blk_id < own, gate, neg_inf)
            chosen = jnp.where(blk_id == own, 1.0, 0.0)
            for _ in range(min(MOBA_TOPK, nblk)):
                best = jnp.max(gate, axis=0, keepdims=True)
                first = jnp.min(jnp.where(gate == best, blk_id, float(nblk)), axis=0, keepdims=True)
                hit = blk_id == first
                chosen = jnp.where(hit & (best > neg_inf), 1.0, chosen)
                gate = jnp.where(hit, neg_inf, gate)
            rowbias_ref[tile * n_heads + h] = jnp.where(
                chosen > 0.0, slope2(h) * KEY_BLOCK * (blk_id - own), neg_inf)

    n_pos_a = qa + 1

    def position(i, tile):
        t = jnp.asarray(i, jnp.int32) - tile * n_pos_a
        is_own = t == 0
        blk = jnp.where(is_own, qa + tile, t - 1)
        return is_own.astype(jnp.int32), blk

    def score_stage(i, tile, slot):
        kind, blk = position(i, tile)
        maxes = []
        for h in heads:
            qh = qt_ref[0, h * LANES:(h + 1) * LANES, tile * tq:(tile + 1) * tq]
            s = _dot(_key_block(k_ref, blk, h), qh) + tile_ref[h, kind]
            s_ref[slot, h] = s
            maxes.append(jnp.max(s, axis=0, keepdims=True) + rowbias_ref[tile * n_heads + h, pl.ds(blk, 1), :])
        return tuple(maxes)

    def value_stage(i, tile, slot, maxes, states):
        _, blk = position(i, tile)
        new_states = []
        for h in heads:
            m, acc = states[h]
            m_new = jnp.maximum(m, maxes[h])
            shift = m_new - rowbias_ref[tile * n_heads + h, pl.ds(blk, 1), :]
            p = jnp.exp2(s_ref[slot, h] - shift)
            acc = jnp.exp2(m - m_new) * acc + _dot(_value_block(vt_ref, blk, h, V_ROWS), p.astype(BF16))
            new_states.append((m_new, acc))
        return tuple(new_states)

    def finish_tile(tile, states):
        outs = [acc[:HEAD_DIM] / acc[HEAD_DIM:HEAD_DIM + 1] for _, acc in states]
        y_ref[0, tile * tq:(tile + 1) * tq] = jnp.concatenate(outs, axis=0).T.astype(BF16)

    def body(ii, carry, tile=0, hand_over=False):
        maxes0, states = carry
        i = 2 * ii
        next_tile = 1 if hand_over else tile
        maxes1 = score_stage(i + 1, next_tile, 1)
        states = value_stage(i, tile, 0, maxes0, states)
        if hand_over:
            finish_tile(tile, states)
            states = init
        maxes2 = score_stage(i + 2, next_tile, 0)
        states = value_stage(i + 1, next_tile, 1, maxes1, states)
        return maxes2, states

    init = tuple((jnp.full((1, tq), neg_inf, F32), jnp.zeros((V_ROWS, tq), F32)) for _ in heads)
    mid = qa // 2
    carry = lax.fori_loop(0, mid, functools.partial(body, tile=0), (score_stage(0, 0, 0), init))
    carry = body(mid, carry, tile=0, hand_over=True)
    maxes0, states = lax.fori_loop(mid + 1, n_pos_a, functools.partial(body, tile=1), carry)
    finish_tile(1, value_stage(2 * n_pos_a, 1, 0, maxes0, states))


def _attention_specs(b, t, v_rows_per_head=HEAD_DIM, tiles_per_step=1):
    nblk = t // KEY_BLOCK
    width = HEADS_PER_STEP * HEAD_DIM
    v_rows = HEADS_PER_STEP * v_rows_per_head
    tq = tiles_per_step * Q_TILE
    in_specs = [
        pl.BlockSpec((1, HEADS_PER_STEP * LANES, tq), lambda bi, hp, qi: (bi, hp, qi)),
        pl.BlockSpec((1, t, width), lambda bi, hp, qi: (bi, 0, hp)),
        pl.BlockSpec((1, nblk, v_rows, KEY_BLOCK), lambda bi, hp, qi: (bi, 0, hp, 0)),
    ]
    out_spec = pl.BlockSpec((1, tq, width), lambda bi, hp, qi: (bi, qi, hp))
    grid = (b, N_HEADS // HEADS_PER_STEP, t // tq)
    return grid, in_specs, out_spec


def _moba_attention(slopes, qt, k, vt):
    b, t, _ = k.shape
    nblk = t // KEY_BLOCK
    grid, in_specs, out_spec = _attention_specs(b, t, V_ROWS, MOBA_TILES_PER_STEP)
    chains = MOBA_TILES_PER_STEP * HEADS_PER_STEP
    return pl.pallas_call(
        functools.partial(_moba_kernel, nblk=nblk),
        grid=grid,
        in_specs=[pl.BlockSpec(memory_space=pltpu.SMEM)] + in_specs,
        out_specs=out_spec,
        out_shape=jax.ShapeDtypeStruct((b, t, W_ATT), BF16),
        scratch_shapes=[pltpu.VMEM((nblk, HEADS_PER_STEP * HEAD_DIM), F32),
                        pltpu.VMEM((chains, nblk, Q_TILE), F32),
                        pltpu.VMEM((HEADS_PER_STEP, 2, KEY_BLOCK, Q_TILE), F32),
                        pltpu.VMEM((2, HEADS_PER_STEP, KEY_BLOCK, Q_TILE), F32)],
        compiler_params=pltpu.CompilerParams(
            dimension_semantics=("parallel", "parallel", "arbitrary"), vmem_limit_bytes=VMEM_LIMIT),
        name="moba_attention",
    )(slopes, qt, k, vt)


def _softplus2(z2):
    return jnp.maximum(z2, 0.0) + jnp.log2(1.0 + jnp.exp2(-jnp.abs(z2)))


def _sb_terms(z2, mask=None):
    sp = _softplus2(z2)
    if mask is not None:
        sp = jnp.where(mask, sp, 0.0)
    return z2 - sp, sp.astype(BF16), sp[0:1, :]


def _sb_kernel(qt_ref, k_ref, vt_ref, y_ref):
    qi = pl.program_id(2)
    tq = Q_TILE
    qt = qt_ref[0]
    key_in_blk = lax.broadcasted_iota(jnp.int32, (KEY_BLOCK, tq), 0)
    qry_in_blk = lax.broadcasted_iota(jnp.int32, (KEY_BLOCK, tq), 1)
    strict = key_in_blk < qry_in_blk
    r = lax.broadcasted_iota(jnp.int32, (KEY_BLOCK, KEY_BLOCK), 0)
    c = lax.broadcasted_iota(jnp.int32, (KEY_BLOCK, KEY_BLOCK), 1)
    later_keys = jnp.where(c > r, 1.0, 0.0).astype(BF16)

    heads = range(HEADS_PER_STEP)
    qhs = [_head_queries(qt, h) for h in heads]

    has_prev = qi > 0
    prev = jnp.maximum(qi - 1, 0)
    z_own = [_dot(_key_block(k_ref, qi, h), qhs[h]) for h in heads]
    z_prev = [_dot(_key_block(k_ref, prev, h), qhs[h]) for h in heads]
    terms_own = [_sb_terms(z_own[h], strict) for h in heads]
    terms_prev = [_sb_terms(z_prev[h]) for h in heads]
    tail_own = [_dot(later_keys, terms_own[h][1]) for h in heads]
    tail_prev = [_dot(later_keys, terms_prev[h][1]) for h in heads]
    accs, carries = [], []
    for h in heads:
        (logb_own, _, sp0_own), (logb_prev, _, sp0_prev) = terms_own[h], terms_prev[h]
        w_own = jnp.where(strict, jnp.exp2(logb_own - tail_own[h]), 0.0)
        carry = jnp.where(has_prev, tail_own[h][0:1, :] + sp0_own, jnp.inf)
        w_prev = jnp.exp2(logb_prev - tail_prev[h] - carry)
        accs.append(_dot(_value_block(vt_ref, qi, h), w_own.astype(BF16))
                    + _dot(_value_block(vt_ref, prev, h), w_prev.astype(BF16)))
        carries.append(carry + tail_prev[h][0:1, :] + sp0_prev)

    def carry_min(carries):
        return jnp.min(functools.reduce(jnp.minimum, carries))

    def cond(state):
        j, cmin, _, _ = state
        return (j >= 0) & (cmin < SB_LOG2_WEIGHT_FLOOR)

    def body(state):
        j, _, carries, accs = state
        new_carries, new_accs = [], []
        for h in heads:
            logb, sp_b, sp0 = _sb_terms(_dot(_key_block(k_ref, j, h), qhs[h]))
            tail = _dot(later_keys, sp_b) + carries[h]
            w = jnp.exp2(logb - tail)
            new_accs.append(accs[h] + _dot(_value_block(vt_ref, j, h), w.astype(BF16)))
            new_carries.append(tail[0:1, :] + sp0)
        return j - 1, carry_min(new_carries), tuple(new_carries), tuple(new_accs)

    _, _, _, accs = lax.while_loop(cond, body, (qi - 2, carry_min(carries), tuple(carries), tuple(accs)))

    y_ref[0] = jnp.concatenate(accs, axis=0).T.astype(BF16)


def _sb_attention(qt, k, vt):
    b, t, _ = k.shape
    grid, in_specs, out_spec = _attention_specs(b, t)
    return pl.pallas_call(
        _sb_kernel,
        grid=grid,
        in_specs=in_specs,
        out_specs=out_spec,
        out_shape=jax.ShapeDtypeStruct((b, t, W_ATT), BF16),
        compiler_params=pltpu.CompilerParams(
            dimension_semantics=("parallel", "parallel", "arbitrary"), vmem_limit_bytes=VMEM_LIMIT),
        name="stickbreak_attention",
    )(qt, k, vt)


def _post_kernel(x_ref, ya_ref, yb_ref, gate_ref, wa_ref, wb_ref, wo_ref, g_ref, wup_ref, cw_ref, cb_ref,
                 wdown_ref, gfin_ref, o_ref, tail_ref, act_ref, *, tiles_per_seq, final_norm):
    tm, d = x_ref.shape
    d_ff = wdown_ref.shape[0]

    @pl.when(pl.program_id(0) % tiles_per_seq == 0)
    def _():
        tail_ref[...] = jnp.zeros_like(tail_ref)

    ga = gate_ref[:, :d].astype(F32)
    gb = gate_ref[:, d:].astype(F32)
    mixed = ga * _dot(ya_ref[...], wa_ref[...]) + gb * _dot(yb_ref[...], wb_ref[...])
    x = x_ref[...] + _dot(mixed.astype(BF16), wo_ref[...])
    o_ref[...] = x
    h = _rms_normalize(x, g_ref[...]).astype(BF16)
    row = lax.broadcasted_iota(jnp.int32, (tm, FF_CHUNK), 0)

    def conv_chunk(lo):
        u = _dot(h, wup_ref[:, lo:lo + FF_CHUNK])
        prev = tail_ref[:, lo:lo + FF_CHUNK]
        tail_ref[:, lo:lo + FF_CHUNK] = u[tm - TAIL_ROWS:, :]
        p1 = prev[TAIL_ROWS - 1:TAIL_ROWS, :]
        p2 = prev[TAIL_ROWS - 2:TAIL_ROWS - 1, :]
        u1 = jnp.where(row == 0, p1, pltpu.roll(u, 1, 0))
        u2 = jnp.where(row == 0, p2, jnp.where(row == 1, p1, pltpu.roll(u, 2, 0)))
        cw = cw_ref[:, lo:lo + FF_CHUNK]
        return cw[0:1] * u2 + cw[1:2] * u1 + cw[2:3] * u + cb_ref[:, lo:lo + FF_CHUNK]

    for c in range(d_ff // FF_CHUNK):
        a = conv_chunk(c * FF_CHUNK)
        bgate = conv_chunk(d_ff + c * FF_CHUNK)
        act_ref[:, c * FF_CHUNK:(c + 1) * FF_CHUNK] = (a * jax.nn.sigmoid(a) * bgate).astype(BF16)

    y = o_ref[...] + _dot(act_ref[...], wdown_ref[...])
    if final_norm:
        y = _rms_normalize(y, gfin_ref[...])
    o_ref[...] = y


def _post(x2, ya, yb, gate, wa, wb, wo, g, wup, cw, cb, wdown, gfin, *, seq_len, final_norm):
    n, d = x2.shape
    d_ff = wdown.shape[0]
    tm = ROW_TILE
    row = lambda width: pl.BlockSpec((tm, width), lambda i: (i, 0))
    full = lambda a: pl.BlockSpec(a.shape, lambda i: (0, 0), pipeline_mode=pl.Buffered(1))
    return pl.pallas_call(
        functools.partial(_post_kernel, tiles_per_seq=seq_len // tm, final_norm=final_norm),
        grid=(n // tm,),
        in_specs=[row(d), row(W_ATT), row(W_ATT), row(2 * d), full(wa), full(wb), full(wo),
                  full(g), full(wup), full(cw), full(cb), full(wdown), full(gfin)],
        out_specs=row(d),
        out_shape=jax.ShapeDtypeStruct((n, d), F32),
        scratch_shapes=[pltpu.VMEM((TAIL_ROWS, 2 * d_ff), F32), pltpu.VMEM((tm, d_ff), BF16)],
        compiler_params=pltpu.CompilerParams(
            dimension_semantics=("arbitrary",), vmem_limit_bytes=VMEM_LIMIT),
        name="mix_conv_ffn",
    )(x2, ya, yb, gate, wa, wb, wo, g, wup, cw, cb, wdown, gfin)


def kernel(x, g_mix, w_in, w_proj_moba, w_proj_sb, w_out, g_ffn, w_up, conv_w, conv_b, w_down, g_final):
    b, t, d = x.shape
    depth = w_in.shape[0]
    assert t % ROW_TILE == 0 and ROW_TILE % KEY_BLOCK == 0 and Q_TILE == KEY_BLOCK
    assert MOBA_TILES_PER_STEP == 2 and t % (MOBA_TILES_PER_STEP * Q_TILE) == 0
    assert w_up.shape[2] % (2 * FF_CHUNK) == 0 and conv_w.shape[1] == CONV_WIDTH
    slopes = jnp.exp2(-8.0 * jnp.arange(1, N_HEADS + 1, dtype=F32) / N_HEADS)
    gfin = g_final.reshape(1, d)
    for layer in range(depth):
        qta, ka, vta, qtb, kb, vtb, gate = _inproj(x, g_mix[layer], w_in[layer].astype(BF16))
        ya = _moba_attention(slopes, qta, ka, vta)
        yb = _sb_attention(qtb, kb, vtb)
        x = _post(x.reshape(b * t, d), ya.reshape(b * t, W_ATT), yb.reshape(b * t, W_ATT),
                  gate.reshape(b * t, 2 * d), w_proj_moba[layer].astype(BF16),
                  w_proj_sb[layer].astype(BF16), w_out[layer].astype(BF16),
                  g_ffn[layer].reshape(1, d), w_up[layer].astype(BF16),
                  conv_w[layer].reshape(CONV_WIDTH, -1), conv_b[layer].reshape(1, -1),
                  w_down[layer].astype(BF16), gfin, seq_len=t,
                  final_norm=(layer == depth - 1)).reshape(b, t, d)
    return x
```

```python
import functools

import jax
import jax.numpy as jnp
from jax import lax
from jax.experimental import pallas as pl
from jax.experimental.pallas import tpu as pltpu

F32 = jnp.float32
BF16 = jnp.bfloat16

HEAD_DIM = 64
N_HEADS = 8
W_ATT = N_HEADS * HEAD_DIM
LANES = 128
HEADS_PER_GROUP = LANES // HEAD_DIM
HEADS_PER_STEP = 4
SB_HEADS_PER_STEP = 8
KEY_BLOCK = 256
Q_TILE = 256
MOBA_TOPK = 3
ONES_ROWS = 16
V_ROWS = HEAD_DIM + ONES_ROWS
MOBA_TILES_PER_STEP = 2
LOG2E = 1.4426950408889634
RMS_EPS = 1e-6
CONV_WIDTH = 3
FF_CHUNK = 256
ROW_TILE = 512
TAIL_ROWS = 8
SB_LOG2_WEIGHT_FLOOR = 160.0
VMEM_LIMIT = 56 * 1024 * 1024


def _dot(a, b):
    return jnp.dot(a, b, preferred_element_type=F32)


def _rms_normalize(x, g):
    return x * lax.rsqrt(jnp.mean(x * x, axis=-1, keepdims=True) + RMS_EPS) * g


def _inproj_kernel(x_ref, g_ref, w_ref, qa_ref, ka_ref, va_ref, qb_ref, kb_ref, vb_ref, gate_ref):
    h = _rms_normalize(x_ref[0], g_ref[...]).astype(BF16)
    scale = HEAD_DIM ** -0.5

    def proj(lo, hi):
        return _dot(h, w_ref[:, lo:hi])

    def put_blocks(ref, t):
        for j in range(t.shape[1] // KEY_BLOCK):
            ref[0, j] = t[:, j * KEY_BLOCK:(j + 1) * KEY_BLOCK].astype(BF16)

    def with_ones_rows(t):
        ones = jnp.ones((ONES_ROWS, t.shape[1]), F32)
        parts = []
        for hd in range(N_HEADS):
            parts += [t[hd * HEAD_DIM:(hd + 1) * HEAD_DIM], ones]
        return jnp.concatenate(parts, axis=0)

    w = W_ATT
    gate_ref[0] = jax.nn.sigmoid(proj(6 * w, w_ref.shape[1])).astype(BF16)
    put_blocks(va_ref, with_ones_rows(proj(2 * w, 3 * w).T))
    put_blocks(vb_ref, proj(5 * w, 6 * w).T)
    qa_ref[0] = (proj(0, w) * (scale * LOG2E)).T.astype(BF16)
    qb_ref[0] = (proj(3 * w, 4 * w) * (scale * LOG2E)).T.astype(BF16)
    ka_ref[0] = proj(w, 2 * w).astype(BF16)
    kb_ref[0] = proj(4 * w, 5 * w).astype(BF16)


def _inproj(x, g, w_in):
    b, t, d = x.shape
    d_in = w_in.shape[1]
    tm = ROW_TILE
    nblk = t // KEY_BLOCK
    qt_shape = jax.ShapeDtypeStruct((b, W_ATT, t), BF16)
    k_shape = jax.ShapeDtypeStruct((b, t, W_ATT), BF16)
    gate_shape = jax.ShapeDtypeStruct((b, t, d_in - 6 * W_ATT), BF16)
    qt_spec = pl.BlockSpec((1, W_ATT, tm), lambda bi, i: (bi, 0, i))
    k_spec = pl.BlockSpec((1, tm, W_ATT), lambda bi, i: (bi, i, 0))

    def vt(rows_per_head):
        rows = N_HEADS * rows_per_head
        return (jax.ShapeDtypeStruct((b, nblk, rows, KEY_BLOCK), BF16),
                pl.BlockSpec((1, tm // KEY_BLOCK, rows, KEY_BLOCK), lambda bi, i: (bi, i, 0, 0)))

    vta_shape, vta_spec = vt(HEAD_DIM + ONES_ROWS)
    vtb_shape, vtb_spec = vt(HEAD_DIM)
    return pl.pallas_call(
        _inproj_kernel,
        grid=(b, t // tm),
        in_specs=[
            pl.BlockSpec((1, tm, d), lambda bi, i: (bi, i, 0)),
            pl.BlockSpec((1, d), lambda bi, i: (0, 0)),
            pl.BlockSpec((d, d_in), lambda bi, i: (0, 0)),
        ],
        out_specs=[qt_spec, k_spec, vta_spec, qt_spec, k_spec, vtb_spec,
                   pl.BlockSpec((1, tm, d_in - 6 * W_ATT), lambda bi, i: (bi, i, 0))],
        out_shape=[qt_shape, k_shape, vta_shape, qt_shape, k_shape, vtb_shape, gate_shape],
        compiler_params=pltpu.CompilerParams(
            dimension_semantics=("parallel", "parallel"), vmem_limit_bytes=VMEM_LIMIT),
        name="inproj",
    )(x, g.reshape(1, d), w_in)


def _lane_group(h):
    g = h // HEADS_PER_GROUP
    return slice(g * LANES, (g + 1) * LANES)


def _head_queries(qt, h):
    grp = qt[_lane_group(h), :].astype(F32)
    row = lax.broadcasted_iota(jnp.int32, grp.shape, 0)
    lo = (h % HEADS_PER_GROUP) * HEAD_DIM
    keep = (row >= lo) & (row < lo + HEAD_DIM)
    return jnp.where(keep, grp, 0.0).astype(BF16)


def _key_block(k_ref, j, h):
    rows = pl.ds(pl.multiple_of(j * KEY_BLOCK, KEY_BLOCK), KEY_BLOCK)
    return k_ref[0, rows, _lane_group(h)]


def _value_block(vt_ref, j, h, rows_per_head=HEAD_DIM):
    return vt_ref[0, j, h * rows_per_head:(h + 1) * rows_per_head, :]


def _moba_kernel(slopes_ref, qt_ref, k_ref, vt_ref, y_ref, kmean_ref, rowbias_ref, tile_ref, s_ref, *, nblk):
    hp = pl.program_id(1)
    qa = MOBA_TILES_PER_STEP * pl.program_id(2)
    tq = Q_TILE

    heads = range(HEADS_PER_STEP)
    n_heads = HEADS_PER_STEP
    neg_inf = -jnp.inf

    def slope2(h):
        return slopes_ref[hp * HEADS_PER_STEP + h] * LOG2E

    @pl.when(qa == 0)
    def _():
        def body(j, c):
            rows = pl.ds(pl.multiple_of(j * KEY_BLOCK, KEY_BLOCK), KEY_BLOCK)
            blk = k_ref[0, rows, :].astype(F32)
            kmean_ref[pl.ds(j, 1), :] = jnp.sum(blk, axis=0, keepdims=True) * (1.0 / KEY_BLOCK)
            return c
        lax.fori_loop(0, nblk, body, 0)
        key_in_blk = lax.broadcasted_iota(jnp.int32, (KEY_BLOCK, tq), 0)
        qry_in_blk = lax.broadcasted_iota(jnp.int32, (KEY_BLOCK, tq), 1)
        rel = (key_in_blk - qry_in_blk).astype(F32)
        for h in heads:
            alibi = slope2(h) * rel
            tile_ref[h, 0] = alibi
            tile_ref[h, 1] = jnp.where(key_in_blk <= qry_in_blk, alibi, neg_inf)

    km = kmean_ref[...]
    km_hi = km.astype(BF16)
    km_r = km - km_hi.astype(F32)
    km_mid = km_r.astype(BF16)
    km_lo = (km_r - km_mid.astype(F32)).astype(BF16)

    blk_id = lax.broadcasted_iota(jnp.int32, (nblk, tq), 0).astype(F32)
    qhs = [[] for _ in range(MOBA_TILES_PER_STEP)]
    for tile in range(MOBA_TILES_PER_STEP):
        qt = qt_ref[0, :, tile * tq:(tile + 1) * tq]
        own = (qa + tile).astype(F32)
        for h in heads:
            qh = _head_queries(qt, h)
            grp = _lane_group(h)
            gate = _dot(km_hi[:, grp], qh) + _dot(km_mid[:, grp], qh) + _dot(km_lo[:, grp], qh)
            gate = jnp.where(blk_id < own, gate, neg_inf)
            chosen = jnp.where(blk_id == own, 1.0, 0.0)
            for _ in range(min(MOBA_TOPK, nblk)):
                best = jnp.max(gate, axis=0, keepdims=True)
                first = jnp.min(jnp.where(gate == best, blk_id, float(nblk)), axis=0, keepdims=True)
                hit = blk_id == first
                chosen = jnp.where(hit & (best > neg_inf), 1.0, chosen)
                gate = jnp.where(hit, neg_inf, gate)
            rowbias_ref[tile * n_heads + h] = jnp.where(
                chosen > 0.0, slope2(h) * KEY_BLOCK * (blk_id - own), neg_inf)
            qhs[tile].append(qh)

    n_pos_a = qa + 1

    def position(i, tile):
        t = jnp.asarray(i, jnp.int32) - tile * n_pos_a
        is_own = t == 0
        blk = jnp.where(is_own, qa + tile, t - 1)
        return is_own.astype(jnp.int32), blk

    def score_stage(i, tile, slot):
        kind, blk = position(i, tile)
        maxes = []
        for h in heads:
            s = _dot(_key_block(k_ref, blk, h), qhs[tile][h]) + tile_ref[h, kind]
            s_ref[slot, h] = s
            maxes.append(jnp.max(s, axis=0, keepdims=True) + rowbias_ref[tile * n_heads + h, pl.ds(blk, 1), :])
        return tuple(maxes)

    def value_stage(i, tile, slot, maxes, states):
        _, blk = position(i, tile)
        new_states = []
        for h in heads:
            m, acc = states[h]
            m_new = jnp.maximum(m, maxes[h])
            shift = m_new - rowbias_ref[tile * n_heads + h, pl.ds(blk, 1), :]
            p = jnp.exp2(s_ref[slot, h] - shift)
            acc = jnp.exp2(m - m_new) * acc + _dot(_value_block(vt_ref, blk, h, V_ROWS), p.astype(BF16))
            new_states.append((m_new, acc))
        return tuple(new_states)

    def finish_tile(tile, states):
        outs = [acc[:HEAD_DIM] / acc[HEAD_DIM:HEAD_DIM + 1] for _, acc in states]
        y_ref[0, tile * tq:(tile + 1) * tq] = jnp.concatenate(outs, axis=0).T.astype(BF16)

    def body(ii, carry, tile=0, hand_over=False):
        maxes0, states = carry
        i = 2 * ii
        next_tile = 1 if hand_over else tile
        maxes1 = score_stage(i + 1, next_tile, 1)
        states = value_stage(i, tile, 0, maxes0, states)
        if hand_over:
            finish_tile(tile, states)
            states = init
        maxes2 = score_stage(i + 2, next_tile, 0)
        states = value_stage(i + 1, next_tile, 1, maxes1, states)
        return maxes2, states

    init = tuple((jnp.full((1, tq), neg_inf, F32), jnp.zeros((V_ROWS, tq), F32)) for _ in heads)
    mid = qa // 2
    carry = lax.fori_loop(0, mid, functools.partial(body, tile=0), (score_stage(0, 0, 0), init))
    carry = body(mid, carry, tile=0, hand_over=True)
    maxes0, states = lax.fori_loop(mid + 1, n_pos_a, functools.partial(body, tile=1), carry)
    finish_tile(1, value_stage(2 * n_pos_a, 1, 0, maxes0, states))


def _attention_specs(b, t, v_rows_per_head=HEAD_DIM, tiles_per_step=1, heads_per_step=HEADS_PER_STEP):
    nblk = t // KEY_BLOCK
    width = heads_per_step * HEAD_DIM
    v_rows = heads_per_step * v_rows_per_head
    tq = tiles_per_step * Q_TILE
    in_specs = [
        pl.BlockSpec((1, width, tq), lambda bi, hp, qi: (bi, hp, qi)),
        pl.BlockSpec((1, t, width), lambda bi, hp, qi: (bi, 0, hp)),
        pl.BlockSpec((1, nblk, v_rows, KEY_BLOCK), lambda bi, hp, qi: (bi, 0, hp, 0)),
    ]
    out_spec = pl.BlockSpec((1, tq, width), lambda bi, hp, qi: (bi, qi, hp))
    grid = (b, N_HEADS // heads_per_step, t // tq)
    return grid, in_specs, out_spec


def _moba_attention(slopes, qt, k, vt):
    b, t, _ = k.shape
    nblk = t // KEY_BLOCK
    grid, in_specs, out_spec = _attention_specs(b, t, V_ROWS, MOBA_TILES_PER_STEP)
    chains = MOBA_TILES_PER_STEP * HEADS_PER_STEP
    return pl.pallas_call(
        functools.partial(_moba_kernel, nblk=nblk),
        grid=grid,
        in_specs=[pl.BlockSpec(memory_space=pltpu.SMEM)] + in_specs,
        out_specs=out_spec,
        out_shape=jax.ShapeDtypeStruct((b, t, W_ATT), BF16),
        scratch_shapes=[pltpu.VMEM((nblk, HEADS_PER_STEP * HEAD_DIM), F32),
                        pltpu.VMEM((chains, nblk, Q_TILE), F32),
                        pltpu.VMEM((HEADS_PER_STEP, 2, KEY_BLOCK, Q_TILE), F32),
                        pltpu.VMEM((2, HEADS_PER_STEP, KEY_BLOCK, Q_TILE), F32)],
        compiler_params=pltpu.CompilerParams(
            dimension_semantics=("parallel", "parallel", "arbitrary"), vmem_limit_bytes=VMEM_LIMIT),
        name="moba_attention",
    )(slopes, qt, k, vt)


def _softplus2(z2):
    return jnp.maximum(z2, 0.0) + jnp.log2(1.0 + jnp.exp2(-jnp.abs(z2)))


def _sb_terms(z2, mask=None):
    sp = _softplus2(z2)
    if mask is not None:
        sp = jnp.where(mask, sp, 0.0)
    return z2 - sp, sp.astype(BF16), sp[0:1, :]


def _sb_kernel(qt_ref, k_ref, vt_ref, y_ref):
    qi = pl.program_id(2)
    tq = Q_TILE
    qt = qt_ref[0]
    key_in_blk = lax.broadcasted_iota(jnp.int32, (KEY_BLOCK, tq), 0)
    qry_in_blk = lax.broadcasted_iota(jnp.int32, (KEY_BLOCK, tq), 1)
    strict = key_in_blk < qry_in_blk
    r = lax.broadcasted_iota(jnp.int32, (KEY_BLOCK, KEY_BLOCK), 0)
    c = lax.broadcasted_iota(jnp.int32, (KEY_BLOCK, KEY_BLOCK), 1)
    later_keys = jnp.where(c > r, 1.0, 0.0).astype(BF16)

    heads = range(SB_HEADS_PER_STEP)
    qhs = [_head_queries(qt, h) for h in heads]

    has_prev = qi > 0
    prev = jnp.maximum(qi - 1, 0)
    z_own = [_dot(_key_block(k_ref, qi, h), qhs[h]) for h in heads]
    z_prev = [_dot(_key_block(k_ref, prev, h), qhs[h]) for h in heads]
    terms_own = [_sb_terms(z_own[h], strict) for h in heads]
    terms_prev = [_sb_terms(z_prev[h]) for h in heads]
    tail_own = [_dot(later_keys, terms_own[h][1]) for h in heads]
    tail_prev = [_dot(later_keys, terms_prev[h][1]) for h in heads]
    accs, carries = [], []
    for h in heads:
        (logb_own, _, sp0_own), (logb_prev, _, sp0_prev) = terms_own[h], terms_prev[h]
        w_own = jnp.where(strict, jnp.exp2(logb_own - tail_own[h]), 0.0)
        carry = jnp.where(has_prev, tail_own[h][0:1, :] + sp0_own, jnp.inf)
        w_prev = jnp.exp2(logb_prev - tail_prev[h] - carry)
        accs.append(_dot(_value_block(vt_ref, qi, h), w_own.astype(BF16))
                    + _dot(_value_block(vt_ref, prev, h), w_prev.astype(BF16)))
        carries.append(carry + tail_prev[h][0:1, :] + sp0_prev)

    def carry_min(carries):
        return jnp.min(functools.reduce(jnp.minimum, carries))

    def cond(state):
        j, cmin, _, _ = state
        return (j >= 0) & (cmin < SB_LOG2_WEIGHT_FLOOR)

    def body(state):
        j, _, carries, accs = state
        new_carries, new_accs = [], []
        for h in heads:
            logb, sp_b, sp0 = _sb_terms(_dot(_key_block(k_ref, j, h), qhs[h]))
            tail = _dot(later_keys, sp_b) + carries[h]
            w = jnp.exp2(logb - tail)
            new_accs.append(accs[h] + _dot(_value_block(vt_ref, j, h), w.astype(BF16)))
            new_carries.append(tail[0:1, :] + sp0)
        return j - 1, carry_min(new_carries), tuple(new_carries), tuple(new_accs)

    _, _, _, accs = lax.while_loop(cond, body, (qi - 2, carry_min(carries), tuple(carries), tuple(accs)))

    y_ref[0] = jnp.concatenate(accs, axis=0).T.astype(BF16)


def _sb_attention(qt, k, vt):
    b, t, _ = k.shape
    grid, in_specs, out_spec = _attention_specs(b, t, heads_per_step=SB_HEADS_PER_STEP)
    return pl.pallas_call(
        _sb_kernel,
        grid=grid,
        in_specs=in_specs,
        out_specs=out_spec,
        out_shape=jax.ShapeDtypeStruct((b, t, W_ATT), BF16),
        compiler_params=pltpu.CompilerParams(
            dimension_semantics=("parallel", "parallel", "arbitrary"), vmem_limit_bytes=VMEM_LIMIT),
        name="stickbreak_attention",
    )(qt, k, vt)


def _post_kernel(x_ref, ya_ref, yb_ref, gate_ref, wa_ref, wb_ref, wo_ref, g_ref, wup_ref, cw_ref, cb_ref,
                 wdown_ref, gfin_ref, o_ref, tail_ref, act_ref, *, tiles_per_seq, final_norm):
    tm, d = x_ref.shape
    d_ff = wdown_ref.shape[0]

    @pl.when(pl.program_id(0) % tiles_per_seq == 0)
    def _():
        tail_ref[...] = jnp.zeros_like(tail_ref)

    ga = gate_ref[:, :d].astype(F32)
    gb = gate_ref[:, d:].astype(F32)
    mixed = ga * _dot(ya_ref[...], wa_ref[...]) + gb * _dot(yb_ref[...], wb_ref[...])
    x = x_ref[...] + _dot(mixed.astype(BF16), wo_ref[...])
    o_ref[...] = x
    h = _rms_normalize(x, g_ref[...]).astype(BF16)
    row = lax.broadcasted_iota(jnp.int32, (tm, FF_CHUNK), 0)

    def conv_chunk(lo):
        u = _dot(h, wup_ref[:, lo:lo + FF_CHUNK])
        prev = tail_ref[:, lo:lo + FF_CHUNK]
        tail_ref[:, lo:lo + FF_CHUNK] = u[tm - TAIL_ROWS:, :]
        p1 = prev[TAIL_ROWS - 1:TAIL_ROWS, :]
        p2 = prev[TAIL_ROWS - 2:TAIL_ROWS - 1, :]
        u1 = jnp.where(row == 0, p1, pltpu.roll(u, 1, 0))
        u2 = jnp.where(row == 0, p2, jnp.where(row == 1, p1, pltpu.roll(u, 2, 0)))
        cw = cw_ref[:, lo:lo + FF_CHUNK]
        return cw[0:1] * u2 + cw[1:2] * u1 + cw[2:3] * u + cb_ref[:, lo:lo + FF_CHUNK]

    for c in range(d_ff // FF_CHUNK):
        a = conv_chunk(c * FF_CHUNK)
        bgate = conv_chunk(d_ff + c * FF_CHUNK)
        act_ref[:, c * FF_CHUNK:(c + 1) * FF_CHUNK] = (a * jax.nn.sigmoid(a) * bgate).astype(BF16)

    y = o_ref[...] + _dot(act_ref[...], wdown_ref[...])
    if final_norm:
        y = _rms_normalize(y, gfin_ref[...])
    o_ref[...] = y


def _post(x2, ya, yb, gate, wa, wb, wo, g, wup, cw, cb, wdown, gfin, *, seq_len, final_norm):
    n, d = x2.shape
    d_ff = wdown.shape[0]
    tm = ROW_TILE
    row = lambda width: pl.BlockSpec((tm, width), lambda i: (i, 0))
    full = lambda a: pl.BlockSpec(a.shape, lambda i: (0, 0), pipeline_mode=pl.Buffered(1))
    return pl.pallas_call(
        functools.partial(_post_kernel, tiles_per_seq=seq_len // tm, final_norm=final_norm),
        grid=(n // tm,),
        in_specs=[row(d), row(W_ATT), row(W_ATT), row(2 * d), full(wa), full(wb), full(wo),
                  full(g), full(wup), full(cw), full(cb), full(wdown), full(gfin)],
        out_specs=row(d),
        out_shape=jax.ShapeDtypeStruct((n, d), F32),
        scratch_shapes=[pltpu.VMEM((TAIL_ROWS, 2 * d_ff), F32), pltpu.VMEM((tm, d_ff), BF16)],
        compiler_params=pltpu.CompilerParams(
            dimension_semantics=("arbitrary",), vmem_limit_bytes=VMEM_LIMIT),
        name="mix_conv_ffn",
    )(x2, ya, yb, gate, wa, wb, wo, g, wup, cw, cb, wdown, gfin)


def kernel(x, g_mix, w_in, w_proj_moba, w_proj_sb, w_out, g_ffn, w_up, conv_w, conv_b, w_down, g_final):
    b, t, d = x.shape
    depth = w_in.shape[0]
    assert t % ROW_TILE == 0 and ROW_TILE % KEY_BLOCK == 0 and Q_TILE == KEY_BLOCK
    assert MOBA_TILES_PER_STEP == 2 and t % (MOBA_TILES_PER_STEP * Q_TILE) == 0
    assert w_up.shape[2] % (2 * FF_CHUNK) == 0 and conv_w.shape[1] == CONV_WIDTH
    slopes = jnp.exp2(-8.0 * jnp.arange(1, N_HEADS + 1, dtype=F32) / N_HEADS)
    gfin = g_final.reshape(1, d)
    for layer in range(depth):
        qta, ka, vta, qtb, kb, vtb, gate = _inproj(x, g_mix[layer], w_in[layer].astype(BF16))
        ya = _moba_attention(slopes, qta, ka, vta)
        yb = _sb_attention(qtb, kb, vtb)
        x = _post(x.reshape(b * t, d), ya.reshape(b * t, W_ATT), yb.reshape(b * t, W_ATT),
                  gate.reshape(b * t, 2 * d), w_proj_moba[layer].astype(BF16),
                  w_proj_sb[layer].astype(BF16), w_out[layer].astype(BF16),
                  g_ffn[layer].reshape(1, d), w_up[layer].astype(BF16),
                  conv_w[layer].reshape(CONV_WIDTH, -1), conv_b[layer].reshape(1, -1),
                  w_down[layer].astype(BF16), gfin, seq_len=t,
                  final_norm=(layer == depth - 1)).reshape(b, t, d)
    return x
```
